```python
import jax, jax.numpy as jnp
from jax import lax
import numpy as np

D_MODEL = 4096
BATCH = 2
SEQ = 4096
DEPTH = 2

HEAD_DIM = 128
BRANCH_WIDTH = D_MODEL // 2
N_HEADS_A = BRANCH_WIDTH // HEAD_DIM
N_HEADS_C = BRANCH_WIDTH // HEAD_DIM
MOBA_BLOCK = 256
MOBA_TOPK = 3
Q_BLOCK = 128
ROPE_THETA = 500000.0
ROPE_DIM = HEAD_DIM // 4
CONV_WIDTH = 3
N_BRANCHES = 3
RMS_EPS = 1e-6
IN_COLS = 12 * BRANCH_WIDTH + N_BRANCHES * D_MODEL

kernel_name = 'hybrid_moba_shortconv_stickbreaking'


def rms_norm(x, g):
    xf = x.astype(jnp.float32)
    y = xf * lax.rsqrt(jnp.mean(xf * xf, axis=-1, keepdims=True) + RMS_EPS)
    return (y * g.astype(jnp.float32)).astype(x.dtype)


def split_heads(t, n_heads):
    b, s, _ = t.shape
    return t.reshape(b, s, n_heads, HEAD_DIM).transpose(0, 2, 1, 3)


def merge_heads(t):
    b, h, s, d = t.shape
    return t.transpose(0, 2, 1, 3).reshape(b, s, h * d)


def partial_rope(x, pos):
    half = ROPE_DIM // 2
    inv_freq = ROPE_THETA ** (-jnp.arange(half, dtype=jnp.float32) / half)
    ang = pos.astype(jnp.float32)[:, None] * inv_freq[None, :]
    cos = jnp.cos(ang).astype(x.dtype)
    sin = jnp.sin(ang).astype(x.dtype)
    x1 = x[..., :half]
    x2 = x[..., half:ROPE_DIM]
    rest = x[..., ROPE_DIM:]
    return jnp.concatenate([x1 * cos - x2 * sin, x2 * cos + x1 * sin, rest], axis=-1)


def moba_attention(q, k, v):
    b, h, s, d = q.shape
    n_blk = -(-s // MOBA_BLOCK)
    s_pad = n_blk * MOBA_BLOCK
    pad = ((0, 0), (0, 0), (0, s_pad - s), (0, 0))
    q = jnp.pad(q, pad)
    k = jnp.pad(k, pad)
    v = jnp.pad(v, pad)
    scale = d ** -0.5
    qf = q.reshape(b * h, s_pad, d)
    kb = k.reshape(b * h, n_blk, MOBA_BLOCK, d)
    vb = v.reshape(b * h, n_blk, MOBA_BLOCK, d)
    k_mean = jnp.mean(kb.astype(jnp.float32), axis=2)
    gate = jnp.einsum('nsd,nbd->nsb', qf.astype(jnp.float32), k_mean)
    q_blk = jnp.arange(s_pad) // MOBA_BLOCK
    past = jnp.arange(n_blk)[None, :] < q_blk[:, None]
    gate = jnp.where(past[None], gate, -jnp.inf)
    n_sel = min(MOBA_TOPK, n_blk)
    _, sel = lax.top_k(gate, n_sel)
    sel_ok = sel < q_blk[None, :, None]
    n_chunks = s_pad // Q_BLOCK
    bh_ids = jnp.repeat(jnp.arange(b * h), n_chunks)
    c_ids = jnp.tile(jnp.arange(n_chunks), b * h)

    def one_chunk(ids):
        bh, c = ids
        start = c * Q_BLOCK
        qc = lax.dynamic_slice_in_dim(qf[bh], start, Q_BLOCK, 0)
        idx = lax.dynamic_slice_in_dim(sel[bh], start, Q_BLOCK, 0)
        ok = lax.dynamic_slice_in_dim(sel_ok[bh], start, Q_BLOCK, 0)
        k_sel = kb[bh, idx]
        v_sel = vb[bh, idx]
        own = start // MOBA_BLOCK
        k_own = kb[bh, own]
        v_own = vb[bh, own]
        s_sel = jnp.einsum('qd,qnkd->qnk', qc, k_sel).astype(jnp.float32) * scale
        s_sel = jnp.where(ok[:, :, None], s_sel, -jnp.inf).reshape(Q_BLOCK, n_sel * MOBA_BLOCK)
        q_pos = start + jnp.arange(Q_BLOCK)
        k_pos = own * MOBA_BLOCK + jnp.arange(MOBA_BLOCK)
        s_own = jnp.einsum('qd,kd->qk', qc, k_own).astype(jnp.float32) * scale
        s_own = jnp.where(k_pos[None, :] <= q_pos[:, None], s_own, -jnp.inf)
        p = jax.nn.softmax(jnp.concatenate([s_sel, s_own], axis=-1), axis=-1).astype(v.dtype)
        p_sel = p[:, :n_sel * MOBA_BLOCK].reshape(Q_BLOCK, n_sel, MOBA_BLOCK)
        p_own = p[:, n_sel * MOBA_BLOCK:]
        return jnp.einsum('qnk,qnkd->qd', p_sel, v_sel) + jnp.einsum('qk,kd->qd', p_own, v_own)

    out = lax.map(one_chunk, (bh_ids, c_ids))
    return out.reshape(b, h, s_pad, d)[:, :, :s]


def stick_breaking_attention(q, k, v):
    b, h, s, d = q.shape
    n_blocks = s // Q_BLOCK
    scale = d ** -0.5
    k_pos = jnp.arange(s)
    qb = q.reshape(b, h, n_blocks, Q_BLOCK, d).transpose(2, 0, 1, 3, 4)

    def one_block(args):
        q_blk, i = args
        z = jnp.einsum('bhqd,bhkd->bhqk', q_blk, k).astype(jnp.float32) * scale
        q_pos = i * Q_BLOCK + jnp.arange(Q_BLOCK)
        causal = k_pos[None, :] < q_pos[:, None]
        log_keep = jnp.where(causal, jax.nn.log_sigmoid(-z), 0.0)
        later = lax.cumsum(log_keep, axis=log_keep.ndim - 1, reverse=True) - log_keep
        a = jnp.where(causal, jnp.exp(jax.nn.log_sigmoid(z) + later), 0.0).astype(v.dtype)
        return jnp.einsum('bhqk,bhkd->bhqd', a, v)

    out = lax.map(one_block, (qb, jnp.arange(n_blocks)))
    return out.transpose(1, 2, 0, 3, 4).reshape(b, h, s, d)


def short_conv_mixer(b_gate, c_gate, u, conv_w):
    xc = c_gate * u
    w = xc.shape[-1]
    y = lax.conv_general_dilated(
        xc, conv_w[:, None, :].astype(xc.dtype), window_strides=(1,),
        padding=[(CONV_WIDTH - 1, 0)], dimension_numbers=('NWC', 'WIO', 'NWC'),
        feature_group_count=w)
    return b_gate * y


def hybrid_layer(x, pre_g, post_g, w_in, b_merge, conv_w, p_a, p_b, p_c, w_o):
    b, s, _ = x.shape
    h = rms_norm(x, pre_g)
    proj = jnp.einsum('bsd,dc->bsc', h, w_in)
    cuts = [BRANCH_WIDTH * i for i in range(1, 13)]
    (aq, ak, av, az, bb, bc, bx, bz, cq, ck, cv, cz, g) = jnp.split(proj, cuts, axis=-1)
    pos = jnp.arange(s)
    qa = partial_rope(split_heads(aq, N_HEADS_A), pos)
    ka = partial_rope(split_heads(ak, N_HEADS_A), pos)
    o_a = merge_heads(moba_attention(qa, ka, split_heads(av, N_HEADS_A))) * jax.nn.silu(az)
    o_b = short_conv_mixer(bb, bc, bx, conv_w) * jax.nn.silu(bz)
    o_c = merge_heads(stick_breaking_attention(
        split_heads(cq, N_HEADS_C), split_heads(ck, N_HEADS_C), split_heads(cv, N_HEADS_C))) * jax.nn.silu(cz)
    gates = jax.nn.sigmoid((g + b_merge).reshape(b, s, N_BRANCHES, D_MODEL))
    y = (gates[:, :, 0] * jnp.einsum('bsw,wd->bsd', o_a, p_a)
         + gates[:, :, 1] * jnp.einsum('bsw,wd->bsd', o_b, p_b)
         + gates[:, :, 2] * jnp.einsum('bsw,wd->bsd', o_c, p_c))
    out = jnp.einsum('bsd,de->bse', y, w_o)
    return x + rms_norm(out, post_g)


def setup_inputs(seed: int = 0) -> dict:
    key = jax.random.key(seed)
    ks = jax.random.split(key, 10)
    f32 = jnp.float32
    x = jax.random.normal(ks[0], (BATCH, SEQ, D_MODEL), f32)
    pre_norm_gain = 1.0 + 0.05 * jax.random.normal(ks[1], (DEPTH, D_MODEL), f32)
    post_norm_gain = 1.0 + 0.05 * jax.random.normal(ks[2], (DEPTH, D_MODEL), f32)
    w_in = jax.random.normal(ks[3], (DEPTH, D_MODEL, IN_COLS), f32) * D_MODEL ** -0.5
    b_merge_gate = 0.01 * jax.random.normal(ks[4], (DEPTH, N_BRANCHES * D_MODEL), f32)
    conv_w = jax.random.normal(ks[5], (DEPTH, CONV_WIDTH, BRANCH_WIDTH), f32) * CONV_WIDTH ** -0.5
    w_branch_a = jax.random.normal(ks[6], (DEPTH, BRANCH_WIDTH, D_MODEL), f32) * BRANCH_WIDTH ** -0.5
    w_branch_b = jax.random.normal(ks[7], (DEPTH, BRANCH_WIDTH, D_MODEL), f32) * BRANCH_WIDTH ** -0.5
    w_branch_c = jax.random.normal(ks[8], (DEPTH, BRANCH_WIDTH, D_MODEL), f32) * BRANCH_WIDTH ** -0.5
    w_out = jax.random.normal(ks[9], (DEPTH, D_MODEL, D_MODEL), f32) * D_MODEL ** -0.5
    return {'x': x, 'pre_norm_gain': pre_norm_gain, 'post_norm_gain': post_norm_gain,
            'w_in': w_in, 'b_merge_gate': b_merge_gate, 'conv_w': conv_w,
            'w_branch_a': w_branch_a, 'w_branch_b': w_branch_b, 'w_branch_c': w_branch_c,
            'w_out': w_out}


def reference(x, pre_norm_gain, post_norm_gain, w_in, b_merge_gate, conv_w,
              w_branch_a, w_branch_b, w_branch_c, w_out):
    for layer in range(DEPTH):
        x = hybrid_layer(x, pre_norm_gain[layer], post_norm_gain[layer], w_in[layer],
                         b_merge_gate[layer], conv_w[layer], w_branch_a[layer],
                         w_branch_b[layer], w_branch_c[layer], w_out[layer])
    return x
```

```python
import functools

import jax
import jax.numpy as jnp
from jax import lax
from jax.experimental import pallas as pl
from jax.experimental.pallas import tpu as pltpu

HEAD_DIM = 128
MOBA_BLOCK = 256
MOBA_TOPK = 3
ROPE_THETA = 500000.0
ROPE_DIM = HEAD_DIM // 4
CONV_WIDTH = 3
N_BRANCHES = 3
RMS_EPS = 1e-6

V7X_VMEM_BYTES = 64 * 1024 * 1024
VMEM_REQUEST_CAP = 56 * 1024 * 1024
SUBLANES = 8

F32 = jnp.float32
BF16 = jnp.bfloat16


def _vmem_limit(block_bytes):
    return int(min(VMEM_REQUEST_CAP, max(32 * 1024 * 1024, 2 * block_bytes + 8 * 1024 * 1024)))


def _tile(n, target):
    if n <= target:
        return n
    t = target - target % 128
    while t >= 128:
        if n % t == 0:
            return t
        t -= 128
    return n


def _sigmoid(x):
    return 1.0 / (1.0 + jnp.exp(-x))


def _silu(x):
    return x * _sigmoid(x)


def _rmsnorm_kernel(x_ref, g_ref, o_ref):
    x = x_ref[...]
    inv = lax.rsqrt(jnp.mean(x * x, axis=-1, keepdims=True) + RMS_EPS)
    o_ref[...] = (x * inv * g_ref[...]).astype(o_ref.dtype)


def _rmsnorm(x2d, gain):
    m, d = x2d.shape
    bm = _tile(m, 256)
    return pl.pallas_call(
        _rmsnorm_kernel,
        grid=(m // bm,),
        in_specs=[pl.BlockSpec((bm, d), lambda i: (i, 0)),
                  pl.BlockSpec((1, d), lambda i: (0, 0))],
        out_specs=pl.BlockSpec((bm, d), lambda i: (i, 0)),
        out_shape=jax.ShapeDtypeStruct((m, d), BF16),
        compiler_params=pltpu.CompilerParams(dimension_semantics=("arbitrary",)),
        name="rmsnorm_pre",
    )(x2d, gain.reshape(1, d))


def _matmul_kernel(x_ref, w_ref, o_ref):
    o_ref[...] = jnp.dot(x_ref[...], w_ref[...], preferred_element_type=F32).astype(o_ref.dtype)


def _in_proj(h, w):
    m, k = h.shape
    _, n = w.shape
    bm = _tile(m, 1024)
    bn = _tile(n, 1024)
    blocks = bm * k * 2 + k * bn * 2 + bm * bn * 4
    return pl.pallas_call(
        _matmul_kernel,
        grid=(m // bm, n // bn),
        in_specs=[pl.BlockSpec((bm, k), lambda i, j: (i, 0)),
                  pl.BlockSpec((k, bn), lambda i, j: (0, j))],
        out_specs=pl.BlockSpec((bm, bn), lambda i, j: (i, j)),
        out_shape=jax.ShapeDtypeStruct((m, n), F32),
        compiler_params=pltpu.CompilerParams(
            dimension_semantics=("arbitrary", "arbitrary"),
            vmem_limit_bytes=_vmem_limit(blocks)),
        name="in_proj",
    )(h, w)


def _rope_tables(s):
    half = ROPE_DIM // 2
    inv_freq = ROPE_THETA ** (-jnp.arange(half, dtype=F32) / half)
    ang = jnp.arange(s).astype(F32)[:, None] * inv_freq[None, :]
    cos = jnp.cos(ang)
    sin = jnp.sin(ang)
    rest = HEAD_DIM - ROPE_DIM
    zeros_h = jnp.zeros((s, half), F32)
    cos_t = jnp.concatenate([cos, cos, jnp.ones((s, rest), F32)], axis=-1)
    sin_up = jnp.concatenate([zeros_h, sin, jnp.zeros((s, rest), F32)], axis=-1)
    sin_dn = jnp.concatenate([-sin, zeros_h, jnp.zeros((s, rest), F32)], axis=-1)
    return cos_t, sin_up, sin_dn


def _rope(x, cos_t, sin_up, sin_dn):
    half = ROPE_DIM // 2
    return (x * cos_t + pltpu.roll(x, half, 1) * sin_up
            + pltpu.roll(x, HEAD_DIM - half, 1) * sin_dn)


_TRANS_B = (((1,), (1,)), ((), ()))


def _moba_kernel(q_ref, k_ref, v_ref, z_ref, cq_ref, uq_ref, dq_ref, ck_ref, uk_ref, dk_ref,
                 o_ref, kr_scr, vb_scr, kmean_scr, *, n_blk, tq):
    i = pl.program_id(2)
    blk = MOBA_BLOCK
    scale = HEAD_DIM ** -0.5

    @pl.when(i == 0)
    def _prepare_keys():
        def one_block(j, carry):
            rows = pl.ds(pl.multiple_of(j * blk, blk), blk)
            kr = _rope(k_ref[rows, :], ck_ref[rows, :], uk_ref[rows, :], dk_ref[rows, :])
            kr_scr[rows, :] = kr.astype(BF16)
            vb_scr[rows, :] = v_ref[rows, :].astype(BF16)
            kmean_scr[pl.ds(j, 1), :] = jnp.mean(kr, axis=0, keepdims=True)
            return carry
        lax.fori_loop(0, n_blk, one_block, 0)

    q = _rope(q_ref[...], cq_ref[...], uq_ref[...], dq_ref[...])
    own = (i * tq) // blk

    gate = lax.dot_general(q, kmean_scr[...], _TRANS_B, precision=lax.Precision.HIGHEST,
                           preferred_element_type=F32)
    lane = lax.broadcasted_iota(jnp.int32, (tq, n_blk), 1)
    past = lane < own
    g = jnp.where(past, gate, -jnp.inf)
    sel = jnp.zeros((tq, n_blk), jnp.bool_)
    for _ in range(min(MOBA_TOPK, n_blk)):
        gmax = jnp.max(g, axis=-1, keepdims=True)
        first = jnp.min(jnp.where(g == gmax, lane, n_blk), axis=-1, keepdims=True)
        pick = lane == first
        sel = jnp.logical_or(sel, pick)
        g = jnp.where(pick, -jnp.inf, g)
    sel_f = jnp.where(jnp.logical_and(sel, past), 1.0, 0.0).astype(F32)

    qb = (q * scale).astype(BF16)

    own_rows = pl.ds(pl.multiple_of(own * blk, blk), blk)
    s = lax.dot_general(qb, kr_scr[own_rows, :], _TRANS_B, preferred_element_type=F32)
    q_pos = i * tq + lax.broadcasted_iota(jnp.int32, (tq, blk), 0)
    k_pos = own * blk + lax.broadcasted_iota(jnp.int32, (tq, blk), 1)
    s = jnp.where(k_pos <= q_pos, s, -jnp.inf)
    m0 = jnp.max(s, axis=-1, keepdims=True)
    p = jnp.exp(s - m0)
    l0 = jnp.sum(p, axis=-1, keepdims=True)
    acc0 = jnp.dot(p.astype(BF16), vb_scr[own_rows, :], preferred_element_type=F32)

    def past_block(j, carry):
        m, l, acc = carry
        rows = pl.ds(pl.multiple_of(j * blk, blk), blk)
        s = lax.dot_general(qb, kr_scr[rows, :], _TRANS_B, preferred_element_type=F32)
        chosen = jnp.max(jnp.where(lane == j, sel_f, 0.0), axis=-1, keepdims=True)
        s = jnp.where(chosen > 0.5, s, -jnp.inf)
        m_new = jnp.maximum(m, jnp.max(s, axis=-1, keepdims=True))
        alpha = jnp.exp(m - m_new)
        p = jnp.exp(s - m_new)
        l = alpha * l + jnp.sum(p, axis=-1, keepdims=True)
        acc = alpha * acc + jnp.dot(p.astype(BF16), vb_scr[rows, :], preferred_element_type=F32)
        return m_new, l, acc

    _, l, acc = lax.fori_loop(0, own, past_block, (m0, l0, acc0))
    o_ref[...] = (acc / l * _silu(z_ref[...])).astype(o_ref.dtype)


def _moba(proj, tables, b, s, width, col_q, col_k, col_v, col_z):
    n_heads = width // HEAD_DIM
    n_blk = s // MOBA_BLOCK
    tq = MOBA_BLOCK
    nq = s // tq
    cos_t, sin_up, sin_dn = tables
    d = HEAD_DIM

    q_spec = lambda off: pl.BlockSpec((tq, d), lambda bi, h, i: (bi * nq + i, off + h))
    kv_spec = lambda off: pl.BlockSpec((s, d), lambda bi, h, i: (bi, off + h))
    tq_spec = pl.BlockSpec((tq, d), lambda bi, h, i: (i, 0))
    tk_spec = pl.BlockSpec((s, d), lambda bi, h, i: (0, 0))
    blocks = 2 * s * d * 4 + 3 * s * d * 4 + 6 * tq * d * 4 + 2 * s * d * 2
    return pl.pallas_call(
        functools.partial(_moba_kernel, n_blk=n_blk, tq=tq),
        grid=(b, n_heads, nq),
        in_specs=[q_spec(col_q), kv_spec(col_k), kv_spec(col_v), q_spec(col_z),
                  tq_spec, tq_spec, tq_spec, tk_spec, tk_spec, tk_spec],
        out_specs=pl.BlockSpec((tq, d), lambda bi, h, i: (bi * nq + i, h)),
        out_shape=jax.ShapeDtypeStruct((b * s, width), BF16),
        scratch_shapes=[pltpu.VMEM((s, d), BF16), pltpu.VMEM((s, d), BF16),
                        pltpu.VMEM((n_blk, d), F32)],
        compiler_params=pltpu.CompilerParams(
            dimension_semantics=("arbitrary", "arbitrary", "arbitrary"),
            vmem_limit_bytes=_vmem_limit(blocks)),
        name="moba_attention",
    )(proj, proj, proj, proj, cos_t, sin_up, sin_dn, cos_t, sin_up, sin_dn)


def _softplus(z):
    return jnp.maximum(z, 0.0) + jnp.log(1.0 + jnp.exp(-jnp.abs(z)))


def _suffix_sums(log_keep, tri):
    hi = log_keep.astype(BF16)
    lo = (log_keep - hi.astype(F32)).astype(BF16)
    return (jnp.dot(hi, tri, preferred_element_type=F32)
            + jnp.dot(lo, tri, preferred_element_type=F32))


def _stick_kernel(q_ref, k_ref, v_ref, z_ref, tri_ref, o_ref, kb_scr, vb_scr, *, tq):
    i = pl.program_id(2)
    scale = HEAD_DIM ** -0.5
    n_tiles = k_ref.shape[0] // tq

    @pl.when(i == 0)
    def _prepare_keys():
        def one_tile(j, carry):
            rows = pl.ds(pl.multiple_of(j * tq, tq), tq)
            kb_scr[rows, :] = k_ref[rows, :].astype(BF16)
            vb_scr[rows, :] = v_ref[rows, :].astype(BF16)
            return carry
        lax.fori_loop(0, n_tiles, one_tile, 0)

    qb = (q_ref[...] * scale).astype(BF16)
    tri = tri_ref[...]

    rows = pl.ds(pl.multiple_of(i * tq, tq), tq)
    z = lax.dot_general(qb, kb_scr[rows, :], _TRANS_B, preferred_element_type=F32)
    causal = (lax.broadcasted_iota(jnp.int32, (tq, tq), 1)
              < lax.broadcasted_iota(jnp.int32, (tq, tq), 0))
    log_keep = jnp.where(causal, -_softplus(z), 0.0)
    later = _suffix_sums(log_keep, tri)
    a = jnp.where(causal, jnp.exp(z + log_keep + later), 0.0)
    acc0 = jnp.dot(a.astype(BF16), vb_scr[rows, :], preferred_element_type=F32)
    carry0 = jnp.sum(log_keep, axis=-1, keepdims=True)

    def earlier_tile(t, state):
        carry, acc = state
        j = i - 1 - t
        rows = pl.ds(pl.multiple_of(j * tq, tq), tq)
        z = lax.dot_general(qb, kb_scr[rows, :], _TRANS_B, preferred_element_type=F32)
        log_keep = -_softplus(z)
        later = _suffix_sums(log_keep, tri) + carry
        a = jnp.exp(z + log_keep + later)
        acc = acc + jnp.dot(a.astype(BF16), vb_scr[rows, :], preferred_element_type=F32)
        carry = carry + jnp.sum(log_keep, axis=-1, keepdims=True)
        return carry, acc

    _, acc = lax.fori_loop(0, i, earlier_tile, (carry0, acc0))
    o_ref[...] = (acc * _silu(z_ref[...])).astype(o_ref.dtype)


def _stick(proj, b, s, width, col_q, col_k, col_v, col_z):
    n_heads = width // HEAD_DIM
    tq = min(256, s)
    nq = s // tq
    d = HEAD_DIM
    tri = (jnp.arange(tq)[:, None] > jnp.arange(tq)[None, :]).astype(BF16)

    q_spec = lambda off: pl.BlockSpec((tq, d), lambda bi, h, i: (bi * nq + i, off + h))
    kv_spec = lambda off: pl.BlockSpec((s, d), lambda bi, h, i: (bi, off + h))
    blocks = 2 * s * d * 4 + 3 * tq * d * 4 + tq * tq * 2 + 2 * s * d * 2
    return pl.pallas_call(
        functools.partial(_stick_kernel, tq=tq),
        grid=(b, n_heads, nq),
        in_specs=[q_spec(col_q), kv_spec(col_k), kv_spec(col_v), q_spec(col_z),
                  pl.BlockSpec((tq, tq), lambda bi, h, i: (0, 0))],
        out_specs=pl.BlockSpec((tq, d), lambda bi, h, i: (bi * nq + i, h)),
        out_shape=jax.ShapeDtypeStruct((b * s, width), BF16),
        scratch_shapes=[pltpu.VMEM((s, d), BF16), pltpu.VMEM((s, d), BF16)],
        compiler_params=pltpu.CompilerParams(
            dimension_semantics=("arbitrary", "arbitrary", "arbitrary"),
            vmem_limit_bytes=_vmem_limit(blocks)),
        name="stick_breaking_attention",
    )(proj, proj, proj, proj, tri)


def _conv_kernel(bb_ref, bc_ref, bx_ref, bz_ref, hc_ref, hx_ref, w_ref, o_ref):
    i = pl.program_id(1)
    ts = bc_ref.shape[0]
    xc = bc_ref[...] * bx_ref[...]
    halo = jnp.where(i == 0, 0.0, hc_ref[...] * hx_ref[...])
    full = jnp.concatenate([halo, xc], axis=0)
    prev1 = pltpu.roll(full, 1, 0)[SUBLANES:]
    prev2 = pltpu.roll(full, 2, 0)[SUBLANES:]
    w = w_ref[...]
    y = w[0:1, :] * prev2 + w[1:2, :] * prev1 + w[2:3, :] * xc
    o_ref[...] = (bb_ref[...] * y * _silu(bz_ref[...])).astype(o_ref.dtype)


def _conv_branch(proj, conv_w, b, s, width, col_bb, col_bc, col_bx, col_bz):
    ts = _tile(s, 512)
    tw = _tile(width, 512)
    ns = s // ts
    nw = width // tw
    halo_blocks = ts // SUBLANES

    main = lambda off: pl.BlockSpec((ts, tw), lambda bi, i, j: (bi * ns + i, off * nw + j))
    halo = lambda off: pl.BlockSpec(
        (SUBLANES, tw),
        lambda bi, i, j: (jnp.maximum((bi * ns + i) * halo_blocks - 1, 0), off * nw + j))
    return pl.pallas_call(
        _conv_kernel,
        grid=(b, ns, nw),
        in_specs=[main(col_bb), main(col_bc), main(col_bx), main(col_bz),
                  halo(col_bc), halo(col_bx),
                  pl.BlockSpec((CONV_WIDTH, tw), lambda bi, i, j: (0, j))],
        out_specs=pl.BlockSpec((ts, tw), lambda bi, i, j: (bi * ns + i, j)),
        out_shape=jax.ShapeDtypeStruct((b * s, width), BF16),
        compiler_params=pltpu.CompilerParams(
            dimension_semantics=("arbitrary", "arbitrary", "arbitrary")),
        name="short_conv_branch",
    )(proj, proj, proj, proj, proj, proj, conv_w)


def _merge_kernel(ua_ref, ub_ref, uc_ref, pa_ref, pb_ref, pc_ref,
                  ga_ref, gb_ref, gc_ref, ba_ref, bb_ref, bc_ref, o_ref):
    def branch(u_ref, p_ref, g_ref, b_ref):
        gate = _sigmoid(g_ref[...] + b_ref[...])
        return gate * jnp.dot(u_ref[...], p_ref[...], preferred_element_type=F32)
    y = (branch(ua_ref, pa_ref, ga_ref, ba_ref) + branch(ub_ref, pb_ref, gb_ref, bb_ref)
         + branch(uc_ref, pc_ref, gc_ref, bc_ref))
    o_ref[...] = y.astype(o_ref.dtype)


def _merge(ua, ub, uc, pa, pb, pc, proj, b_merge, d_model, col_g):
    m, width = ua.shape
    bm = _tile(m, 512)
    bn = _tile(d_model, 512)
    nn = d_model // bn

    u_spec = pl.BlockSpec((bm, width), lambda i, j: (i, 0))
    p_spec = pl.BlockSpec((width, bn), lambda i, j: (0, j))
    g_spec = lambda br: pl.BlockSpec((bm, bn), lambda i, j: (i, col_g + br * nn + j))
    b_spec = lambda br: pl.BlockSpec((1, bn), lambda i, j: (0, br * nn + j))
    blocks = 3 * bm * width * 2 + 3 * width * bn * 2 + 3 * bm * bn * 4 + bm * bn * 2
    return pl.pallas_call(
        _merge_kernel,
        grid=(m // bm, nn),
        in_specs=[u_spec, u_spec, u_spec, p_spec, p_spec, p_spec,
                  g_spec(0), g_spec(1), g_spec(2), b_spec(0), b_spec(1), b_spec(2)],
        out_specs=pl.BlockSpec((bm, bn), lambda i, j: (i, j)),
        out_shape=jax.ShapeDtypeStruct((m, d_model), BF16),
        compiler_params=pltpu.CompilerParams(
            dimension_semantics=("arbitrary", "arbitrary"),
            vmem_limit_bytes=_vmem_limit(blocks)),
        name="gated_merge",
    )(ua, ub, uc, pa, pb, pc, proj, proj, proj, b_merge, b_merge, b_merge)


def _out_kernel(y_ref, w_ref, x_ref, g_ref, o_ref, *, bn):
    j = pl.program_id(1)
    cols = pl.ds(pl.multiple_of(j * bn, bn), bn)
    o_ref[:, cols] = jnp.dot(y_ref[...], w_ref[...], preferred_element_type=F32)

    @pl.when(j == pl.num_programs(1) - 1)
    def _normalize():
        out = o_ref[...]
        inv = lax.rsqrt(jnp.mean(out * out, axis=-1, keepdims=True) + RMS_EPS)
        o_ref[...] = x_ref[...] + out * inv * g_ref[...]


def _out_proj(y, w_o, x2d, gain):
    m, d = y.shape
    bm = _tile(m, 512)
    bn = _tile(d, 512)
    blocks = bm * d * 2 + d * bn * 2 + 2 * bm * d * 4
    return pl.pallas_call(
        functools.partial(_out_kernel, bn=bn),
        grid=(m // bm, d // bn),
        in_specs=[pl.BlockSpec((bm, d), lambda i, j: (i, 0)),
                  pl.BlockSpec((d, bn), lambda i, j: (0, j)),
                  pl.BlockSpec((bm, d), lambda i, j: (i, 0)),
                  pl.BlockSpec((1, d), lambda i, j: (0, 0))],
        out_specs=pl.BlockSpec((bm, d), lambda i, j: (i, 0)),
        out_shape=jax.ShapeDtypeStruct((m, d), F32),
        compiler_params=pltpu.CompilerParams(
            dimension_semantics=("arbitrary", "arbitrary"),
            vmem_limit_bytes=_vmem_limit(blocks)),
        name="out_proj_norm_residual",
    )(y, w_o, x2d, gain.reshape(1, d))


def _layer(x2d, b, s, tables, pre_g, post_g, w_in, b_merge, conv_w, p_a, p_b, p_c, w_o):
    d_model = x2d.shape[-1]
    width = conv_w.shape[-1]
    hw = width // HEAD_DIM

    h = _rmsnorm(x2d, pre_g)
    proj = _in_proj(h, w_in.astype(BF16))

    ua = _moba(proj, tables, b, s, width, 0 * hw, 1 * hw, 2 * hw, 3 * hw)
    ub = _conv_branch(proj, conv_w, b, s, width, 4, 5, 6, 7)
    uc = _stick(proj, b, s, width, 8 * hw, 9 * hw, 10 * hw, 11 * hw)

    bn = _tile(d_model, 512)
    y = _merge(ua, ub, uc, p_a.astype(BF16), p_b.astype(BF16), p_c.astype(BF16),
               proj, b_merge.reshape(1, -1), d_model, (12 * width) // bn)
    return _out_proj(y, w_o.astype(BF16), x2d, post_g)


def kernel(x, pre_norm_gain, post_norm_gain, w_in, b_merge_gate, conv_w,
           w_branch_a, w_branch_b, w_branch_c, w_out):
    b, s, d_model = x.shape
    depth = w_in.shape[0]
    assert s % MOBA_BLOCK == 0 and d_model % 128 == 0
    tables = _rope_tables(s)
    x2d = x.reshape(b * s, d_model)
    for layer in range(depth):
        x2d = _layer(x2d, b, s, tables, pre_norm_gain[layer], post_norm_gain[layer],
                     w_in[layer], b_merge_gate[layer], conv_w[layer], w_branch_a[layer],
                     w_branch_b[layer], w_branch_c[layer], w_out[layer])
    return x2d.reshape(b, s, d_model)
```

```python
import functools

import jax
import jax.numpy as jnp
from jax import lax
from jax.experimental import pallas as pl
from jax.experimental.pallas import tpu as pltpu

HEAD_DIM = 128
MOBA_BLOCK = 256
MOBA_TOPK = 3
ROPE_THETA = 500000.0
ROPE_DIM = HEAD_DIM // 4
CONV_WIDTH = 3
N_BRANCHES = 3
RMS_EPS = 1e-6

V7X_VMEM_BYTES = 64 * 1024 * 1024
VMEM_REQUEST_CAP = 56 * 1024 * 1024
SUBLANES = 8

F32 = jnp.float32
BF16 = jnp.bfloat16


def _vmem_limit(block_bytes):
    return int(min(VMEM_REQUEST_CAP, max(32 * 1024 * 1024, 2 * block_bytes + 8 * 1024 * 1024)))


def _tile(n, target):
    if n <= target:
        return n
    t = target - target % 128
    while t >= 128:
        if n % t == 0:
            return t
        t -= 128
    return n


def _sigmoid(x):
    return 1.0 / (1.0 + jnp.exp(-x))


def _silu(x):
    return x * _sigmoid(x)


def _rmsnorm_kernel(x_ref, g_ref, o_ref):
    x = x_ref[...]
    inv = lax.rsqrt(jnp.mean(x * x, axis=-1, keepdims=True) + RMS_EPS)
    o_ref[...] = (x * inv * g_ref[...]).astype(o_ref.dtype)


def _rmsnorm(x2d, gain):
    m, d = x2d.shape
    bm = _tile(m, 256)
    return pl.pallas_call(
        _rmsnorm_kernel,
        grid=(m // bm,),
        in_specs=[pl.BlockSpec((bm, d), lambda i: (i, 0)),
                  pl.BlockSpec((1, d), lambda i: (0, 0))],
        out_specs=pl.BlockSpec((bm, d), lambda i: (i, 0)),
        out_shape=jax.ShapeDtypeStruct((m, d), BF16),
        compiler_params=pltpu.CompilerParams(dimension_semantics=("arbitrary",)),
        name="rmsnorm_pre",
    )(x2d, gain.reshape(1, d))


def _matmul_kernel(x_ref, w_ref, o_ref):
    o_ref[...] = jnp.dot(x_ref[...], w_ref[...], preferred_element_type=F32).astype(o_ref.dtype)


def _in_proj(h, w):
    m, k = h.shape
    _, n = w.shape
    bm = _tile(m, 1024)
    bn = _tile(n, 1024)
    blocks = bm * k * 2 + k * bn * 2 + bm * bn * 4
    return pl.pallas_call(
        _matmul_kernel,
        grid=(m // bm, n // bn),
        in_specs=[pl.BlockSpec((bm, k), lambda i, j: (i, 0)),
                  pl.BlockSpec((k, bn), lambda i, j: (0, j))],
        out_specs=pl.BlockSpec((bm, bn), lambda i, j: (i, j)),
        out_shape=jax.ShapeDtypeStruct((m, n), F32),
        compiler_params=pltpu.CompilerParams(
            dimension_semantics=("arbitrary", "arbitrary"),
            vmem_limit_bytes=_vmem_limit(blocks)),
        name="in_proj",
    )(h, w)


def _rope_tables(s):
    half = ROPE_DIM // 2
    inv_freq = ROPE_THETA ** (-jnp.arange(half, dtype=F32) / half)
    ang = jnp.arange(s).astype(F32)[:, None] * inv_freq[None, :]
    cos = jnp.cos(ang)
    sin = jnp.sin(ang)
    rest = HEAD_DIM - ROPE_DIM
    zeros_h = jnp.zeros((s, half), F32)
    cos_t = jnp.concatenate([cos, cos, jnp.ones((s, rest), F32)], axis=-1)
    sin_up = jnp.concatenate([zeros_h, sin, jnp.zeros((s, rest), F32)], axis=-1)
    sin_dn = jnp.concatenate([-sin, zeros_h, jnp.zeros((s, rest), F32)], axis=-1)
    return cos_t, sin_up, sin_dn


def _rope(x, cos_t, sin_up, sin_dn):
    half = ROPE_DIM // 2
    return (x * cos_t + pltpu.roll(x, half, 1) * sin_up
            + pltpu.roll(x, HEAD_DIM - half, 1) * sin_dn)


_TRANS_B = (((1,), (1,)), ((), ()))


HEADS_PER_STEP = 2
MOBA_CHUNK_BLOCKS = 4


def _head_cols(g):
    return slice(g * HEAD_DIM, (g + 1) * HEAD_DIM)


def _moba_kernel(q_ref, k_ref, v_ref, z_ref, cq_ref, uq_ref, dq_ref, ck_ref, uk_ref, dk_ref,
                 o_ref, kr_scr, vt_blk_scr, vt_chunk_scr, kmean_scr, bias_scr,
                 *, n_blk, tq, n_grp, n_cb):
    i = pl.program_id(2)
    blk = MOBA_BLOCK
    chunk = n_cb * blk
    scale = HEAD_DIM ** -0.5
    neg_inf = -jnp.inf
    heads = range(n_grp)

    @pl.when(i == 0)
    def _prepare_keys():
        def one_chunk(c, carry):
            for u in range(n_cb):
                j = c * n_cb + u
                rows = pl.ds(pl.multiple_of(j * blk, blk), blk)
                ck, uk, dk = ck_ref[rows, :], uk_ref[rows, :], dk_ref[rows, :]
                for g in heads:
                    kr = _rope(k_ref[rows, _head_cols(g)], ck, uk, dk)
                    kr_scr[g, rows, :] = kr.astype(BF16)
                    vt = v_ref[rows, _head_cols(g)].T.astype(BF16)
                    vt_blk_scr[g, j] = vt
                    vt_chunk_scr[g, c, :, u * blk:(u + 1) * blk] = vt
                    kmean_scr[g, pl.ds(j, 1), :] = jnp.mean(kr, axis=0, keepdims=True)
            return carry
        lax.fori_loop(0, n_blk // n_cb, one_chunk, 0)

    own = i
    own_rows = pl.ds(pl.multiple_of(own * blk, blk), blk)
    cq, uq, dq = cq_ref[...], uq_ref[...], dq_ref[...]
    blk_id = lax.broadcasted_iota(jnp.int32, (n_blk, tq), 0).astype(F32)
    past = blk_id < own.astype(F32)
    causal = (lax.broadcasted_iota(jnp.int32, (blk, tq), 0)
              <= lax.broadcasted_iota(jnp.int32, (blk, tq), 1))

    qs = [_rope(q_ref[:, _head_cols(g)], cq, uq, dq) for g in heads]
    qbs = [(q * scale).astype(BF16) for q in qs]
    ss = [lax.dot_general(kr_scr[g, own_rows, :], qbs[g], _TRANS_B, preferred_element_type=F32)
          for g in heads]
    gates = [lax.dot_general(kmean_scr[g], qs[g], _TRANS_B, precision=lax.Precision.HIGHEST,
                             preferred_element_type=F32) for g in heads]
    gscs = [jnp.where(past, gate, neg_inf) for gate in gates]
    sels = [jnp.zeros((n_blk, tq), jnp.bool_) for _ in heads]
    for _ in range(min(MOBA_TOPK, n_blk)):
        gmaxs = [jnp.max(gsc, axis=0, keepdims=True) for gsc in gscs]
        firsts = [jnp.min(jnp.where(gscs[g] == gmaxs[g], blk_id, float(n_blk)), axis=0,
                          keepdims=True) for g in heads]
        picks = [blk_id == first for first in firsts]
        sels = [jnp.logical_or(sels[g], picks[g]) for g in heads]
        gscs = [jnp.where(picks[g], neg_inf, gscs[g]) for g in heads]
    for g in heads:
        bias_scr[g] = jnp.where(jnp.logical_and(sels[g], past), 0.0, neg_inf)

    ss = [jnp.where(causal, s, neg_inf) for s in ss]
    ms = [jnp.max(s, axis=0, keepdims=True) for s in ss]
    ps = [jnp.exp(ss[g] - ms[g]) for g in heads]
    ls = [jnp.sum(p, axis=0, keepdims=True) for p in ps]
    accs = [jnp.dot(vt_blk_scr[g, own], ps[g].astype(BF16), preferred_element_type=F32)
            for g in heads]

    def past_chunk(c, state):
        ms, ls, accs = state
        rows = pl.ds(pl.multiple_of(c * chunk, chunk), chunk)
        ss = [lax.dot_general(kr_scr[g, rows, :], qbs[g], _TRANS_B, preferred_element_type=F32)
              for g in heads]
        ss = [jnp.concatenate([ss[g][u * blk:(u + 1) * blk] + bias_scr[g, pl.ds(c * n_cb + u, 1), :]
                               for u in range(n_cb)], axis=0) for g in heads]
        m_news = [jnp.maximum(ms[g], jnp.max(ss[g], axis=0, keepdims=True)) for g in heads]
        alphas = [jnp.exp(ms[g] - m_news[g]) for g in heads]
        ps = [jnp.exp(ss[g] - m_news[g]) for g in heads]
        ls = [alphas[g] * ls[g] + jnp.sum(ps[g], axis=0, keepdims=True) for g in heads]
        accs = [alphas[g] * accs[g] + jnp.dot(vt_chunk_scr[g, c], ps[g].astype(BF16),
                                              preferred_element_type=F32) for g in heads]
        return tuple(m_news), tuple(ls), tuple(accs)

    n_chunks = (own + n_cb - 1) // n_cb
    _, ls, accs = lax.fori_loop(0, n_chunks, past_chunk, (tuple(ms), tuple(ls), tuple(accs)))
    for g in heads:
        out = (accs[g] * (1.0 / ls[g])).T
        o_ref[:, _head_cols(g)] = (out * _silu(z_ref[:, _head_cols(g)])).astype(o_ref.dtype)


def _moba(proj, tables, b, s, width, col_q, col_k, col_v, col_z):
    n_grp = HEADS_PER_STEP
    n_steps = width // (HEAD_DIM * n_grp)
    n_blk = s // MOBA_BLOCK
    n_cb = MOBA_CHUNK_BLOCKS
    assert n_blk % n_cb == 0
    tq = MOBA_BLOCK
    nq = s // tq
    cos_t, sin_up, sin_dn = tables
    d = HEAD_DIM
    dg = d * n_grp
    assert col_q % n_grp == 0 and col_k % n_grp == 0 and col_v % n_grp == 0 and col_z % n_grp == 0

    q_spec = lambda off: pl.BlockSpec((tq, dg), lambda bi, h, i: (bi * nq + i, off // n_grp + h))
    kv_spec = lambda off: pl.BlockSpec((s, dg), lambda bi, h, i: (bi, off // n_grp + h))
    tq_spec = pl.BlockSpec((tq, d), lambda bi, h, i: (i, 0))
    tk_spec = pl.BlockSpec((s, d), lambda bi, h, i: (0, 0))
    blocks = 2 * s * dg * 4 + 3 * s * d * 4 + 2 * tq * dg * 4 + 3 * tq * d * 4 + 3 * s * dg * 2
    return pl.pallas_call(
        functools.partial(_moba_kernel, n_blk=n_blk, tq=tq, n_grp=n_grp, n_cb=n_cb),
        grid=(b, n_steps, nq),
        in_specs=[q_spec(col_q), kv_spec(col_k), kv_spec(col_v), q_spec(col_z),
                  tq_spec, tq_spec, tq_spec, tk_spec, tk_spec, tk_spec],
        out_specs=pl.BlockSpec((tq, dg), lambda bi, h, i: (bi * nq + i, h)),
        out_shape=jax.ShapeDtypeStruct((b * s, width), BF16),
        scratch_shapes=[pltpu.VMEM((n_grp, s, d), BF16),
                        pltpu.VMEM((n_grp, n_blk, d, MOBA_BLOCK), BF16),
                        pltpu.VMEM((n_grp, n_blk // n_cb, d, n_cb * MOBA_BLOCK), BF16),
                        pltpu.VMEM((n_grp, n_blk, d), F32),
                        pltpu.VMEM((n_grp, n_blk, tq), F32)],
        compiler_params=pltpu.CompilerParams(
            dimension_semantics=("arbitrary", "arbitrary", "arbitrary"),
            vmem_limit_bytes=_vmem_limit(blocks)),
        name="moba_attention",
    )(proj, proj, proj, proj, cos_t, sin_up, sin_dn, cos_t, sin_up, sin_dn)


F32_EXP_ZERO_BELOW = -105.0


def _softplus(z):
    return jnp.maximum(z, 0.0) + jnp.log(1.0 + jnp.exp(-jnp.abs(z)))


def _suffix_sums(tri, log_keep):
    hi = log_keep.astype(BF16)
    lo = (log_keep - hi.astype(F32)).astype(BF16)
    return (jnp.dot(tri, hi, preferred_element_type=F32)
            + jnp.dot(tri, lo, preferred_element_type=F32))


def _stick_kernel(q_ref, k_ref, v_ref, z_ref, tri_ref, o_ref, kb_scr, vt_scr, *, tq, n_grp):
    i = pl.program_id(2)
    scale = HEAD_DIM ** -0.5
    n_tiles = k_ref.shape[0] // tq
    neg_inf = -jnp.inf
    heads = range(n_grp)

    @pl.when(i == 0)
    def _prepare_keys():
        def one_tile(j, carry):
            rows = pl.ds(pl.multiple_of(j * tq, tq), tq)
            for g in heads:
                kb_scr[g, rows, :] = k_ref[rows, _head_cols(g)].astype(BF16)
                vt_scr[g, j] = v_ref[rows, _head_cols(g)].T.astype(BF16)
            return carry
        lax.fori_loop(0, n_tiles, one_tile, 0)

    tri = tri_ref[...]
    qbs = [(q_ref[:, _head_cols(g)] * scale).astype(BF16) for g in heads]

    def still_needed(carries):
        worst = carries[0]
        for c in carries[1:]:
            worst = jnp.maximum(worst, c)
        return (jnp.max(worst) > F32_EXP_ZERO_BELOW).astype(jnp.int32)

    first = jnp.maximum(i - 1, 0)
    rows2 = pl.ds(pl.multiple_of(first * tq, tq), 2 * tq)
    key_pos = first * tq + lax.broadcasted_iota(jnp.int32, (2 * tq, tq), 0)
    qry_pos = i * tq + lax.broadcasted_iota(jnp.int32, (2 * tq, tq), 1)
    allowed = key_pos < qry_pos
    zs = [lax.dot_general(kb_scr[g, rows2, :], qbs[g], _TRANS_B, preferred_element_type=F32)
          for g in heads]
    sps = [_softplus(z) for z in zs]
    log_keeps = [jnp.where(allowed, -sps[g], 0.0) for g in heads]
    log_betas = [jnp.where(allowed, zs[g] - sps[g], neg_inf) for g in heads]
    sums_lo = [jnp.sum(lk[:tq], axis=0, keepdims=True) for lk in log_keeps]
    sums_hi = [jnp.sum(lk[tq:], axis=0, keepdims=True) for lk in log_keeps]
    laters_lo = [_suffix_sums(tri, log_keeps[g][:tq]) + sums_hi[g] for g in heads]
    laters_hi = [_suffix_sums(tri, log_keeps[g][tq:]) for g in heads]
    a_lo = [jnp.exp(log_betas[g][:tq] + laters_lo[g]).astype(BF16) for g in heads]
    a_hi = [jnp.exp(log_betas[g][tq:] + laters_hi[g]).astype(BF16) for g in heads]
    accs = [jnp.dot(vt_scr[g, first], a_lo[g], preferred_element_type=F32)
            + jnp.dot(vt_scr[g, first + 1], a_hi[g], preferred_element_type=F32)
            for g in heads]
    carries = [sums_lo[g] + sums_hi[g] for g in heads]

    def more_tiles(state):
        t, go = state[0], state[1]
        return jnp.logical_and(t < first, go > 0)

    def earlier_tile(state):
        t, _, carries, accs = state
        j = first - 1 - t
        rows = pl.ds(pl.multiple_of(j * tq, tq), tq)
        zs = [lax.dot_general(kb_scr[g, rows, :], qbs[g], _TRANS_B, preferred_element_type=F32)
              for g in heads]
        sps = [_softplus(z) for z in zs]
        new_carries = [carries[g] - jnp.sum(sps[g], axis=0, keepdims=True) for g in heads]
        laters = [_suffix_sums(tri, -sps[g]) + carries[g] for g in heads]
        a = [jnp.exp(zs[g] - sps[g] + laters[g]).astype(BF16) for g in heads]
        new_accs = [accs[g] + jnp.dot(vt_scr[g, j], a[g], preferred_element_type=F32)
                    for g in heads]
        return (t + 1, still_needed(new_carries), tuple(new_carries), tuple(new_accs))

    state = lax.while_loop(more_tiles, earlier_tile,
                           (jnp.int32(0), still_needed(carries), tuple(carries), tuple(accs)))
    accs = state[3]
    for g in heads:
        o_ref[:, _head_cols(g)] = (accs[g].T * _silu(z_ref[:, _head_cols(g)])).astype(o_ref.dtype)


def _stick(proj, b, s, width, col_q, col_k, col_v, col_z):
    n_grp = HEADS_PER_STEP
    n_steps = width // (HEAD_DIM * n_grp)
    tq = 256
    assert s % tq == 0 and s >= 2 * tq
    nq = s // tq
    d = HEAD_DIM
    dg = d * n_grp
    assert col_q % n_grp == 0 and col_k % n_grp == 0 and col_v % n_grp == 0 and col_z % n_grp == 0
    tri = (jnp.arange(tq)[None, :] > jnp.arange(tq)[:, None]).astype(BF16)

    q_spec = lambda off: pl.BlockSpec((tq, dg), lambda bi, h, i: (bi * nq + i, off // n_grp + h))
    kv_spec = lambda off: pl.BlockSpec((s, dg), lambda bi, h, i: (bi, off // n_grp + h))
    blocks = 2 * s * dg * 4 + 3 * tq * dg * 4 + tq * tq * 2 + 2 * s * dg * 2
    return pl.pallas_call(
        functools.partial(_stick_kernel, tq=tq, n_grp=n_grp),
        grid=(b, n_steps, nq),
        in_specs=[q_spec(col_q), kv_spec(col_k), kv_spec(col_v), q_spec(col_z),
                  pl.BlockSpec((tq, tq), lambda bi, h, i: (0, 0))],
        out_specs=pl.BlockSpec((tq, dg), lambda bi, h, i: (bi * nq + i, h)),
        out_shape=jax.ShapeDtypeStruct((b * s, width), BF16),
        scratch_shapes=[pltpu.VMEM((n_grp, s, d), BF16),
                        pltpu.VMEM((n_grp, s // tq, d, tq), BF16)],
        compiler_params=pltpu.CompilerParams(
            dimension_semantics=("arbitrary", "arbitrary", "arbitrary"),
            vmem_limit_bytes=_vmem_limit(blocks)),
        name="stick_breaking_attention",
    )(proj, proj, proj, proj, tri)


def _conv_kernel(bb_ref, bc_ref, bx_ref, bz_ref, hc_ref, hx_ref, w_ref, o_ref):
    i = pl.program_id(1)
    xc = bc_ref[...] * bx_ref[...]
    halo = jnp.where(i == 0, 0.0, hc_ref[...] * hx_ref[...])
    full = jnp.concatenate([halo, xc], axis=0)
    prev1 = pltpu.roll(full, 1, 0)[SUBLANES:]
    prev2 = pltpu.roll(full, 2, 0)[SUBLANES:]
    w = w_ref[...]
    y = w[0:1, :] * prev2 + w[1:2, :] * prev1 + w[2:3, :] * xc
    o_ref[...] = (bb_ref[...] * y * _silu(bz_ref[...])).astype(o_ref.dtype)


def _conv_branch(proj, conv_w, b, s, width, col_bb, col_bc, col_bx, col_bz):
    ts = _tile(s, 512)
    tw = _tile(width, 512)
    ns = s // ts
    nw = width // tw
    halo_blocks = ts // SUBLANES

    main = lambda off: pl.BlockSpec((ts, tw), lambda bi, i, j: (bi * ns + i, off * nw + j))
    halo = lambda off: pl.BlockSpec(
        (SUBLANES, tw),
        lambda bi, i, j: (jnp.maximum((bi * ns + i) * halo_blocks - 1, 0), off * nw + j))
    return pl.pallas_call(
        _conv_kernel,
        grid=(b, ns, nw),
        in_specs=[main(col_bb), main(col_bc), main(col_bx), main(col_bz),
                  halo(col_bc), halo(col_bx),
                  pl.BlockSpec((CONV_WIDTH, tw), lambda bi, i, j: (0, j))],
        out_specs=pl.BlockSpec((ts, tw), lambda bi, i, j: (bi * ns + i, j)),
        out_shape=jax.ShapeDtypeStruct((b * s, width), BF16),
        compiler_params=pltpu.CompilerParams(
            dimension_semantics=("arbitrary", "arbitrary", "arbitrary")),
        name="short_conv_branch",
    )(proj, proj, proj, proj, proj, proj, conv_w)


def _merge_kernel(ua_ref, ub_ref, uc_ref, pa_ref, pb_ref, pc_ref,
                  ga_ref, gb_ref, gc_ref, ba_ref, bb_ref, bc_ref, o_ref):
    def branch(u_ref, p_ref, g_ref, b_ref):
        gate = _sigmoid(g_ref[...] + b_ref[...])
        return gate * jnp.dot(u_ref[...], p_ref[...], preferred_element_type=F32)
    y = (branch(ua_ref, pa_ref, ga_ref, ba_ref) + branch(ub_ref, pb_ref, gb_ref, bb_ref)
         + branch(uc_ref, pc_ref, gc_ref, bc_ref))
    o_ref[...] = y.astype(o_ref.dtype)


def _merge(ua, ub, uc, pa, pb, pc, proj, b_merge, d_model, col_g):
    m, width = ua.shape
    bm = _tile(m, 512)
    bn = _tile(d_model, 512)
    nn = d_model // bn

    u_spec = pl.BlockSpec((bm, width), lambda i, j: (i, 0))
    p_spec = pl.BlockSpec((width, bn), lambda i, j: (0, j))
    g_spec = lambda br: pl.BlockSpec((bm, bn), lambda i, j: (i, col_g + br * nn + j))
    b_spec = lambda br: pl.BlockSpec((1, bn), lambda i, j: (0, br * nn + j))
    blocks = 3 * bm * width * 2 + 3 * width * bn * 2 + 3 * bm * bn * 4 + bm * bn * 2
    return pl.pallas_call(
        _merge_kernel,
        grid=(m // bm, nn),
        in_specs=[u_spec, u_spec, u_spec, p_spec, p_spec, p_spec,
                  g_spec(0), g_spec(1), g_spec(2), b_spec(0), b_spec(1), b_spec(2)],
        out_specs=pl.BlockSpec((bm, bn), lambda i, j: (i, j)),
        out_shape=jax.ShapeDtypeStruct((m, d_model), BF16),
        compiler_params=pltpu.CompilerParams(
            dimension_semantics=("arbitrary", "arbitrary"),
            vmem_limit_bytes=_vmem_limit(blocks)),
        name="gated_merge",
    )(ua, ub, uc, pa, pb, pc, proj, proj, proj, b_merge, b_merge, b_merge)


def _out_kernel(y_ref, w_ref, x_ref, g_ref, o_ref, *, bn):
    j = pl.program_id(1)
    cols = pl.ds(pl.multiple_of(j * bn, bn), bn)
    o_ref[:, cols] = jnp.dot(y_ref[...], w_ref[...], preferred_element_type=F32)

    @pl.when(j == pl.num_programs(1) - 1)
    def _normalize():
        out = o_ref[...]
        inv = lax.rsqrt(jnp.mean(out * out, axis=-1, keepdims=True) + RMS_EPS)
        o_ref[...] = x_ref[...] + out * inv * g_ref[...]


def _out_proj(y, w_o, x2d, gain):
    m, d = y.shape
    bm = _tile(m, 512)
    bn = _tile(d, 512)
    blocks = bm * d * 2 + d * bn * 2 + 2 * bm * d * 4
    return pl.pallas_call(
        functools.partial(_out_kernel, bn=bn),
        grid=(m // bm, d // bn),
        in_specs=[pl.BlockSpec((bm, d), lambda i, j: (i, 0)),
                  pl.BlockSpec((d, bn), lambda i, j: (0, j)),
                  pl.BlockSpec((bm, d), lambda i, j: (i, 0)),
                  pl.BlockSpec((1, d), lambda i, j: (0, 0))],
        out_specs=pl.BlockSpec((bm, d), lambda i, j: (i, 0)),
        out_shape=jax.ShapeDtypeStruct((m, d), F32),
        compiler_params=pltpu.CompilerParams(
            dimension_semantics=("arbitrary", "arbitrary"),
            vmem_limit_bytes=_vmem_limit(blocks)),
        name="out_proj_norm_residual",
    )(y, w_o, x2d, gain.reshape(1, d))


def _layer(x2d, b, s, tables, pre_g, post_g, w_in, b_merge, conv_w, p_a, p_b, p_c, w_o):
    d_model = x2d.shape[-1]
    width = conv_w.shape[-1]
    hw = width // HEAD_DIM

    h = _rmsnorm(x2d, pre_g)
    proj = _in_proj(h, w_in.astype(BF16))

    ua = _moba(proj, tables, b, s, width, 0 * hw, 1 * hw, 2 * hw, 3 * hw)
    ub = _conv_branch(proj, conv_w, b, s, width, 4, 5, 6, 7)
    uc = _stick(proj, b, s, width, 8 * hw, 9 * hw, 10 * hw, 11 * hw)

    bn = _tile(d_model, 512)
    y = _merge(ua, ub, uc, p_a.astype(BF16), p_b.astype(BF16), p_c.astype(BF16),
               proj, b_merge.reshape(1, -1), d_model, (12 * width) // bn)
    return _out_proj(y, w_o.astype(BF16), x2d, post_g)


def kernel(x, pre_norm_gain, post_norm_gain, w_in, b_merge_gate, conv_w,
           w_branch_a, w_branch_b, w_branch_c, w_out):
    b, s, d_model = x.shape
    depth = w_in.shape[0]
    assert s % MOBA_BLOCK == 0 and d_model % 128 == 0
    tables = _rope_tables(s)
    x2d = x.reshape(b * s, d_model)
    for layer in range(depth):
        x2d = _layer(x2d, b, s, tables, pre_norm_gain[layer], post_norm_gain[layer],
                     w_in[layer], b_merge_gate[layer], conv_w[layer], w_branch_a[layer],
                     w_branch_b[layer], w_branch_c[layer], w_out[layer])
    return x2d.reshape(b, s, d_model)
```

```python
import functools
import math

import jax
import jax.numpy as jnp
from jax import lax
from jax.experimental import pallas as pl
from jax.experimental.pallas import tpu as pltpu

HEAD_DIM = 128
MOBA_BLOCK = 256
MOBA_TOPK = 3
ROPE_THETA = 500000.0
ROPE_DIM = HEAD_DIM // 4
CONV_WIDTH = 3
N_BRANCHES = 3
RMS_EPS = 1e-6

V7X_VMEM_BYTES = 64 * 1024 * 1024
VMEM_REQUEST_CAP = 60000 * 1024
SUBLANES = 8

F32 = jnp.float32
BF16 = jnp.bfloat16


def _vmem_limit(block_bytes):
    return int(min(VMEM_REQUEST_CAP, max(32 * 1024 * 1024, 2 * block_bytes + 8 * 1024 * 1024)))


def _tile(n, target):
    if n <= target:
        return n
    t = target - target % 128
    while t >= 128:
        if n % t == 0:
            return t
        t -= 128
    return n


def _sigmoid(x):
    return 1.0 / (1.0 + jnp.exp(-x))


def _silu(x):
    return x * _sigmoid(x)


def _rmsnorm_kernel(x_ref, g_ref, o_ref):
    x = x_ref[...]
    inv = lax.rsqrt(jnp.mean(x * x, axis=-1, keepdims=True) + RMS_EPS)
    o_ref[...] = (x * inv * g_ref[...]).astype(o_ref.dtype)


def _rmsnorm(x2d, gain):
    m, d = x2d.shape
    bm = _tile(m, 256)
    return pl.pallas_call(
        _rmsnorm_kernel,
        grid=(m // bm,),
        in_specs=[pl.BlockSpec((bm, d), lambda i: (i, 0)),
                  pl.BlockSpec((1, d), lambda i: (0, 0))],
        out_specs=pl.BlockSpec((bm, d), lambda i: (i, 0)),
        out_shape=jax.ShapeDtypeStruct((m, d), BF16),
        compiler_params=pltpu.CompilerParams(dimension_semantics=("arbitrary",)),
        name="rmsnorm_pre",
    )(x2d, gain.reshape(1, d))


def _rope_tables(s):
    half = ROPE_DIM // 2
    inv_freq = ROPE_THETA ** (-jnp.arange(half, dtype=F32) / half)
    ang = jnp.arange(s).astype(F32)[:, None] * inv_freq[None, :]
    cos = jnp.cos(ang)
    sin = jnp.sin(ang)
    rest = HEAD_DIM - ROPE_DIM
    zeros_h = jnp.zeros((s, half), F32)
    cos_t = jnp.concatenate([cos, cos, jnp.ones((s, rest), F32)], axis=-1)
    sin_up = jnp.concatenate([zeros_h, sin, jnp.zeros((s, rest), F32)], axis=-1)
    sin_dn = jnp.concatenate([-sin, zeros_h, jnp.zeros((s, rest), F32)], axis=-1)
    return cos_t, sin_up, sin_dn


def _rope(x, cos_t, sin_up, sin_dn):
    half = ROPE_DIM // 2
    return (x * cos_t + pltpu.roll(x, half, 1) * sin_up
            + pltpu.roll(x, HEAD_DIM - half, 1) * sin_dn)


def _in_proj_kernel(x_ref, w_hbm, cos_ref, up_ref, dn_ref, o_ref, wb_scr, stage_scr, sem,
                    *, layer, n_rope_tiles):
    j, i = pl.program_id(0), pl.program_id(1)
    n_tiles, n_steps = pl.num_programs(0), pl.num_programs(1)
    _, k, bn = wb_scr.shape
    ks = stage_scr.shape[1]
    n_slices = k // ks
    slot = j % 2

    def slice_copy(tile, sl, buf):
        src = w_hbm.at[layer, pl.ds(pl.multiple_of(sl * ks, ks), ks),
                       pl.ds(pl.multiple_of(tile * bn, bn), bn)]
        return pltpu.make_async_copy(src, stage_scr.at[buf], sem.at[buf])

    def cast_slice(sl, buf, dst_slot):
        wb_scr[dst_slot, pl.ds(pl.multiple_of(sl * ks, ks), ks), :] = stage_scr[buf].astype(BF16)

    @pl.when(jnp.logical_and(j == 0, i == 0))
    def _first_tile():
        slice_copy(0, 0, 0).start()
        for sl in range(n_slices):
            if sl + 1 < n_slices:
                slice_copy(0, sl + 1, (sl + 1) % 2).start()
            slice_copy(0, sl, sl % 2).wait()
            cast_slice(sl, sl % 2, 0)

    has_next = j + 1 < n_tiles

    @pl.when(jnp.logical_and(has_next, i > 0))
    def _finish_previous_slice():
        slice_copy(j + 1, i - 1, (i - 1) % 2).wait()
        cast_slice(i - 1, (i - 1) % 2, 1 - slot)

    @pl.when(has_next)
    def _fetch_slice():
        slice_copy(j + 1, i, i % 2).start()

    o_ref[...] = jnp.dot(x_ref[...], wb_scr[slot], preferred_element_type=F32)

    @pl.when(jnp.logical_and(has_next, i == n_steps - 1))
    def _finish_last_slice():
        slice_copy(j + 1, i, i % 2).wait()
        cast_slice(i, i % 2, 1 - slot)

    @pl.when(j < n_rope_tiles)
    def _rotary():
        cos_t, sin_up, sin_dn = cos_ref[...], up_ref[...], dn_ref[...]
        for h in range(o_ref.shape[1] // HEAD_DIM):
            cols = slice(h * HEAD_DIM, (h + 1) * HEAD_DIM)
            o_ref[:, cols] = _rope(o_ref[:, cols], cos_t, sin_up, sin_dn)


def _in_proj(h, w_stack, layer, tables, rope_cols):
    m, k = h.shape
    _, _, n = w_stack.shape
    s = tables[0].shape[0]
    bm = _tile(min(m, s), 1024)
    bn = _tile(math.gcd(n, rope_cols), 1024)
    n_steps = m // bm
    assert s % bm == 0 and rope_cols % bn == 0 and n % bn == 0 and k % (n_steps * SUBLANES) == 0
    ks = k // n_steps
    n_rope_tiles = rope_cols // bn
    pos_blocks = s // bm
    table_spec = pl.BlockSpec(
        (bm, HEAD_DIM), lambda j, i: (jnp.where(j < n_rope_tiles, i % pos_blocks, 0), 0))
    return pl.pallas_call(
        functools.partial(_in_proj_kernel, layer=layer, n_rope_tiles=n_rope_tiles),
        grid=(n // bn, n_steps),
        in_specs=[pl.BlockSpec((bm, k), lambda j, i: (i, 0)),
                  pl.BlockSpec(memory_space=pl.ANY),
                  table_spec, table_spec, table_spec],
        out_specs=pl.BlockSpec((bm, bn), lambda j, i: (i, j)),
        out_shape=jax.ShapeDtypeStruct((m, n), F32),
        scratch_shapes=[pltpu.VMEM((2, k, bn), BF16),
                        pltpu.VMEM((2, ks, bn), F32),
                        pltpu.SemaphoreType.DMA((2,))],
        compiler_params=pltpu.CompilerParams(
            dimension_semantics=("arbitrary", "arbitrary"),
            vmem_limit_bytes=VMEM_REQUEST_CAP),
        name="in_proj",
    )(h, w_stack, *tables)


_TRANS_B = (((1,), (1,)), ((), ()))


MOBA_HEADS_PER_STEP = 2
STICK_HEADS_PER_STEP = 4
MOBA_CHUNK_BLOCKS = 4
LOG2_E = 1.4426950408889634


def _head_cols(g):
    return slice(g * HEAD_DIM, (g + 1) * HEAD_DIM)


def _moba_kernel(q_ref, k_ref, v_ref, z_ref, o_ref, kr_scr, vt_blk_scr, vt_chunk_scr, kmean_scr,
                 bias_scr, s0_scr, s1_scr, *, n_blk, tq, n_grp, n_cb):
    i = pl.program_id(2)
    blk = MOBA_BLOCK
    chunk = n_cb * blk
    scale2 = HEAD_DIM ** -0.5 * LOG2_E
    neg_inf = -jnp.inf
    heads = range(n_grp)

    @pl.when(i == 0)
    def _prepare_keys():
        def one_chunk(c, carry):
            for u in range(n_cb):
                j = c * n_cb + u
                rows = pl.ds(pl.multiple_of(j * blk, blk), blk)
                for g in heads:
                    kr = k_ref[rows, _head_cols(g)]
                    kr_scr[g, rows, :] = kr.astype(BF16)
                    vt = v_ref[rows, _head_cols(g)].T.astype(BF16)
                    vt_blk_scr[g, j] = vt
                    vt_chunk_scr[g, c, :, u * blk:(u + 1) * blk] = vt
                    kmean_scr[g, pl.ds(j, 1), :] = jnp.mean(kr, axis=0, keepdims=True)
            return carry
        lax.fori_loop(0, n_blk // n_cb, one_chunk, 0)

    own = i
    own_rows = pl.ds(pl.multiple_of(own * blk, blk), blk)
    blk_id = lax.broadcasted_iota(jnp.int32, (n_blk, tq), 0).astype(F32)
    past = blk_id < own.astype(F32)
    causal = (lax.broadcasted_iota(jnp.int32, (blk, tq), 0)
              <= lax.broadcasted_iota(jnp.int32, (blk, tq), 1))

    qs = [q_ref[:, _head_cols(g)] for g in heads]
    qbs = [(q * scale2).astype(BF16) for q in qs]
    ss = [lax.dot_general(kr_scr[g, own_rows, :], qbs[g], _TRANS_B, preferred_element_type=F32)
          for g in heads]
    gates = [lax.dot_general(kmean_scr[g], qs[g], _TRANS_B, precision=lax.Precision.HIGHEST,
                             preferred_element_type=F32) for g in heads]
    gscs = [jnp.where(past, gate, neg_inf) for gate in gates]
    sels = [jnp.zeros((n_blk, tq), jnp.bool_) for _ in heads]
    for _ in range(min(MOBA_TOPK, n_blk)):
        gmaxs = [jnp.max(gsc, axis=0, keepdims=True) for gsc in gscs]
        firsts = [jnp.min(jnp.where(gscs[g] == gmaxs[g], blk_id, float(n_blk)), axis=0,
                          keepdims=True) for g in heads]
        picks = [blk_id == first for first in firsts]
        sels = [jnp.logical_or(sels[g], picks[g]) for g in heads]
        gscs = [jnp.where(picks[g], neg_inf, gscs[g]) for g in heads]
    for g in heads:
        bias_scr[g] = jnp.where(jnp.logical_and(sels[g], past), 0.0, neg_inf)

    ss = [jnp.where(causal, s, neg_inf) for s in ss]
    ms = [jnp.max(s, axis=0, keepdims=True) for s in ss]
    ps = [jnp.exp2(ss[g] - ms[g]) for g in heads]
    ls = [jnp.sum(p, axis=0, keepdims=True) for p in ps]
    accs = [jnp.dot(vt_blk_scr[g, own], ps[g].astype(BF16), preferred_element_type=F32)
            for g in heads]

    n_chunks = (own + n_cb - 1) // n_cb
    last_chunk = n_blk // n_cb - 1

    def score_chunk(c, dst_scr):
        c = jnp.minimum(c, last_chunk)
        rows = pl.ds(pl.multiple_of(c * chunk, chunk), chunk)
        for g in heads:
            dst_scr[g] = lax.dot_general(kr_scr[g, rows, :], qbs[g], _TRANS_B,
                                         preferred_element_type=F32)

    def absorb_chunk(c, src_scr, state):
        ms, ls, accs = state
        m_news, new_ls, new_accs = [], [], []
        for g in heads:
            s = jnp.concatenate(
                [src_scr[g, u * blk:(u + 1) * blk, :] + bias_scr[g, pl.ds(c * n_cb + u, 1), :]
                 for u in range(n_cb)], axis=0)
            m_new = jnp.maximum(ms[g], jnp.max(s, axis=0, keepdims=True))
            alpha = jnp.exp2(ms[g] - m_new)
            p = jnp.exp2(s - m_new)
            new_ls.append(alpha * ls[g] + jnp.sum(p, axis=0, keepdims=True))
            new_accs.append(alpha * accs[g] + jnp.dot(vt_chunk_scr[g, c], p.astype(BF16),
                                                      preferred_element_type=F32))
            m_news.append(m_new)
        return tuple(m_news), tuple(new_ls), tuple(new_accs)

    score_chunk(0, s0_scr)

    def chunk_pair(p, state):
        c = 2 * p
        score_chunk(c + 1, s1_scr)
        state = absorb_chunk(c, s0_scr, state)

        def second(state):
            score_chunk(c + 2, s0_scr)
            return absorb_chunk(c + 1, s1_scr, state)

        return lax.cond(c + 1 < n_chunks, second, lambda st: st, state)

    _, ls, accs = lax.fori_loop(0, (n_chunks + 1) // 2, chunk_pair,
                                (tuple(ms), tuple(ls), tuple(accs)))
    for g in heads:
        out = (accs[g] * (1.0 / ls[g])).T
        o_ref[:, _head_cols(g)] = (out * _silu(z_ref[:, _head_cols(g)])).astype(o_ref.dtype)


def _moba(proj, b, s, width, col_q, col_k, col_v, col_z):
    n_grp = MOBA_HEADS_PER_STEP
    n_steps = width // (HEAD_DIM * n_grp)
    n_blk = s // MOBA_BLOCK
    n_cb = MOBA_CHUNK_BLOCKS
    assert n_blk % n_cb == 0
    tq = MOBA_BLOCK
    nq = s // tq
    d = HEAD_DIM
    dg = d * n_grp
    assert col_q % n_grp == 0 and col_k % n_grp == 0 and col_v % n_grp == 0 and col_z % n_grp == 0

    q_spec = lambda off: pl.BlockSpec((tq, dg), lambda bi, h, i: (bi * nq + i, off // n_grp + h))
    kv_spec = lambda off: pl.BlockSpec((s, dg), lambda bi, h, i: (bi, off // n_grp + h))
    blocks = 2 * s * dg * 4 + 2 * tq * dg * 4 + tq * dg * 2 + 3 * s * dg * 2
    return pl.pallas_call(
        functools.partial(_moba_kernel, n_blk=n_blk, tq=tq, n_grp=n_grp, n_cb=n_cb),
        grid=(b, n_steps, nq),
        in_specs=[q_spec(col_q), kv_spec(col_k), kv_spec(col_v), q_spec(col_z)],
        out_specs=pl.BlockSpec((tq, dg), lambda bi, h, i: (bi * nq + i, h)),
        out_shape=jax.ShapeDtypeStruct((b * s, width), BF16),
        scratch_shapes=[pltpu.VMEM((n_grp, s, d), BF16),
                        pltpu.VMEM((n_grp, n_blk, d, MOBA_BLOCK), BF16),
                        pltpu.VMEM((n_grp, n_blk // n_cb, d, n_cb * MOBA_BLOCK), BF16),
                        pltpu.VMEM((n_grp, n_blk, d), F32),
                        pltpu.VMEM((n_grp, n_blk, tq), F32),
                        pltpu.VMEM((n_grp, n_cb * MOBA_BLOCK, tq), F32),
                        pltpu.VMEM((n_grp, n_cb * MOBA_BLOCK, tq), F32)],
        compiler_params=pltpu.CompilerParams(
            dimension_semantics=("arbitrary", "arbitrary", "arbitrary"),
            vmem_limit_bytes=_vmem_limit(blocks + n_grp * n_cb * MOBA_BLOCK * tq * 4)),
        name="moba_attention",
    )(proj, proj, proj, proj)


F32_EXP2_ZERO_BELOW = -152.0


def _softplus2(z2):
    return jnp.maximum(z2, 0.0) + jnp.log2(1.0 + jnp.exp2(-jnp.abs(z2)))


def _suffix_sums(tri, x):
    hi = x.astype(BF16)
    lo = (x - hi.astype(F32)).astype(BF16)
    return (jnp.dot(tri, hi, preferred_element_type=F32)
            + jnp.dot(tri, lo, preferred_element_type=F32))


def _stick_kernel(q_ref, k_ref, v_ref, z_ref, tri_ref, o_ref, kb_scr, vt_scr, *, tq, n_grp):
    i = pl.program_id(2)
    scale2 = HEAD_DIM ** -0.5 * LOG2_E
    n_tiles = k_ref.shape[0] // tq
    heads = range(n_grp)

    @pl.when(i == 0)
    def _prepare_keys():
        def one_tile(j, carry):
            rows = pl.ds(pl.multiple_of(j * tq, tq), tq)
            for g in heads:
                kb_scr[g, rows, :] = k_ref[rows, _head_cols(g)].astype(BF16)
                vt_scr[g, j] = v_ref[rows, _head_cols(g)].T.astype(BF16)
            return carry
        lax.fori_loop(0, n_tiles, one_tile, 0)

    tri = tri_ref[...]
    qbs = [(q_ref[:, _head_cols(g)] * scale2).astype(BF16) for g in heads]
    causal = (lax.broadcasted_iota(jnp.int32, (tq, tq), 0)
              < lax.broadcasted_iota(jnp.int32, (tq, tq), 1))

    rows = pl.ds(pl.multiple_of(i * tq, tq), tq)
    zs = [lax.dot_general(kb_scr[g, rows, :], qbs[g], _TRANS_B, preferred_element_type=F32)
          for g in heads]
    sps = [_softplus2(z) for z in zs]
    drops = [jnp.where(causal, sp, 0.0) for sp in sps]
    removed = [jnp.sum(dr, axis=0, keepdims=True) for dr in drops]
    laters = [_suffix_sums(tri, dr) for dr in drops]
    a = [jnp.where(causal, jnp.exp2(zs[g] - sps[g] - laters[g]), 0.0).astype(BF16) for g in heads]
    accs = [jnp.dot(vt_scr[g, i], a[g], preferred_element_type=F32) for g in heads]

    def still_needed(removed):
        least = removed[0]
        for r in removed[1:]:
            least = jnp.minimum(least, r)
        return (-jnp.min(least) > F32_EXP2_ZERO_BELOW).astype(jnp.int32)

    def more_tiles(state):
        t, go = state[0], state[1]
        return jnp.logical_and(t < i, go > 0)

    def earlier_tile(state):
        t, _, removed, accs = state
        j = i - 1 - t
        rows = pl.ds(pl.multiple_of(j * tq, tq), tq)
        zs = [lax.dot_general(kb_scr[g, rows, :], qbs[g], _TRANS_B, preferred_element_type=F32)
              for g in heads]
        sps = [_softplus2(z) for z in zs]
        new_removed = [removed[g] + jnp.sum(sps[g], axis=0, keepdims=True) for g in heads]
        laters = [_suffix_sums(tri, sps[g]) + removed[g] for g in heads]
        a = [jnp.exp2(zs[g] - sps[g] - laters[g]).astype(BF16) for g in heads]
        new_accs = [accs[g] + jnp.dot(vt_scr[g, j], a[g], preferred_element_type=F32)
                    for g in heads]
        return (t + 1, still_needed(new_removed), tuple(new_removed), tuple(new_accs))

    state = lax.while_loop(more_tiles, earlier_tile,
                           (jnp.int32(0), jnp.int32(1), tuple(removed), tuple(accs)))
    for g in heads:
        o_ref[:, _head_cols(g)] = (state[3][g].T * _silu(z_ref[:, _head_cols(g)])).astype(o_ref.dtype)


def _stick(proj, b, s, width, col_q, col_k, col_v, col_z):
    n_grp = math.gcd(STICK_HEADS_PER_STEP, width // HEAD_DIM)
    n_steps = width // (HEAD_DIM * n_grp)
    tq = 256
    assert s % tq == 0
    nq = s // tq
    d = HEAD_DIM
    dg = d * n_grp
    assert col_q % n_grp == 0 and col_k % n_grp == 0 and col_v % n_grp == 0 and col_z % n_grp == 0
    tri = (jnp.arange(tq)[None, :] > jnp.arange(tq)[:, None]).astype(BF16)

    q_spec = lambda off: pl.BlockSpec((tq, dg), lambda bi, h, i: (bi * nq + i, off // n_grp + h))
    kv_spec = lambda off: pl.BlockSpec((s, dg), lambda bi, h, i: (bi, off // n_grp + h))
    blocks = 2 * s * dg * 4 + 3 * tq * dg * 4 + tq * tq * 2 + 2 * s * dg * 2
    return pl.pallas_call(
        functools.partial(_stick_kernel, tq=tq, n_grp=n_grp),
        grid=(b, n_steps, nq),
        in_specs=[q_spec(col_q), kv_spec(col_k), kv_spec(col_v), q_spec(col_z),
                  pl.BlockSpec((tq, tq), lambda bi, h, i: (0, 0))],
        out_specs=pl.BlockSpec((tq, dg), lambda bi, h, i: (bi * nq + i, h)),
        out_shape=jax.ShapeDtypeStruct((b * s, width), BF16),
        scratch_shapes=[pltpu.VMEM((n_grp, s, d), BF16),
                        pltpu.VMEM((n_grp, s // tq, d, tq), BF16)],
        compiler_params=pltpu.CompilerParams(
            dimension_semantics=("arbitrary", "arbitrary", "arbitrary"),
            vmem_limit_bytes=_vmem_limit(blocks)),
        name="stick_breaking_attention",
    )(proj, proj, proj, proj, tri)


def _conv_kernel(bb_ref, bc_ref, bx_ref, bz_ref, hc_ref, hx_ref, w_ref, o_ref):
    i = pl.program_id(1)
    xc = bc_ref[...] * bx_ref[...]
    halo = jnp.where(i == 0, 0.0, hc_ref[...] * hx_ref[...])
    full = jnp.concatenate([halo, xc], axis=0)
    prev1 = pltpu.roll(full, 1, 0)[SUBLANES:]
    prev2 = pltpu.roll(full, 2, 0)[SUBLANES:]
    w = w_ref[...]
    y = w[0:1, :] * prev2 + w[1:2, :] * prev1 + w[2:3, :] * xc
    o_ref[...] = (bb_ref[...] * y * _silu(bz_ref[...])).astype(o_ref.dtype)


def _conv_branch(proj, conv_w, b, s, width, col_bb, col_bc, col_bx, col_bz):
    ts = _tile(s, 512)
    tw = _tile(width, 512)
    ns = s // ts
    nw = width // tw
    halo_blocks = ts // SUBLANES

    main = lambda off: pl.BlockSpec((ts, tw), lambda bi, i, j: (bi * ns + i, off * nw + j))
    halo = lambda off: pl.BlockSpec(
        (SUBLANES, tw),
        lambda bi, i, j: (jnp.maximum((bi * ns + i) * halo_blocks - 1, 0), off * nw + j))
    return pl.pallas_call(
        _conv_kernel,
        grid=(b, ns, nw),
        in_specs=[main(col_bb), main(col_bc), main(col_bx), main(col_bz),
                  halo(col_bc), halo(col_bx),
                  pl.BlockSpec((CONV_WIDTH, tw), lambda bi, i, j: (0, j))],
        out_specs=pl.BlockSpec((ts, tw), lambda bi, i, j: (bi * ns + i, j)),
        out_shape=jax.ShapeDtypeStruct((b * s, width), BF16),
        compiler_params=pltpu.CompilerParams(
            dimension_semantics=("arbitrary", "arbitrary", "arbitrary")),
        name="short_conv_branch",
    )(proj, proj, proj, proj, proj, proj, conv_w)


def _merge_kernel(ua_ref, ub_ref, uc_ref, pa_ref, pb_ref, pc_ref,
                  ga_ref, gb_ref, gc_ref, ba_ref, bb_ref, bc_ref, o_ref,
                  pa_scr, pb_scr, pc_scr):
    @pl.when(pl.program_id(1) == 0)
    def _cast_weights():
        pa_scr[...] = pa_ref[...].astype(BF16)
        pb_scr[...] = pb_ref[...].astype(BF16)
        pc_scr[...] = pc_ref[...].astype(BF16)

    def branch(u_ref, p_scr, g_ref, b_ref):
        gate = _sigmoid(g_ref[...] + b_ref[...])
        return gate * jnp.dot(u_ref[...], p_scr[...], preferred_element_type=F32)
    y = (branch(ua_ref, pa_scr, ga_ref, ba_ref) + branch(ub_ref, pb_scr, gb_ref, bb_ref)
         + branch(uc_ref, pc_scr, gc_ref, bc_ref))
    o_ref[...] = y.astype(o_ref.dtype)


def _merge(ua, ub, uc, pa, pb, pc, layer, proj, b_merge, d_model, col_g):
    m, width = ua.shape
    bm = _tile(m, 512)
    bn = _tile(d_model, 512)
    nn = d_model // bn

    u_spec = pl.BlockSpec((bm, width), lambda j, i: (i, 0))
    p_spec = pl.BlockSpec((None, width, bn), lambda j, i: (layer, 0, j))
    g_spec = lambda br: pl.BlockSpec((bm, bn), lambda j, i: (i, col_g + br * nn + j))
    b_spec = lambda br: pl.BlockSpec((1, bn), lambda j, i: (0, br * nn + j))
    blocks = 3 * bm * width * 2 + 3 * width * bn * 4 + 3 * bm * bn * 4 + bm * bn * 2
    return pl.pallas_call(
        _merge_kernel,
        grid=(nn, m // bm),
        in_specs=[u_spec, u_spec, u_spec, p_spec, p_spec, p_spec,
                  g_spec(0), g_spec(1), g_spec(2), b_spec(0), b_spec(1), b_spec(2)],
        out_specs=pl.BlockSpec((bm, bn), lambda j, i: (i, j)),
        out_shape=jax.ShapeDtypeStruct((m, d_model), BF16),
        scratch_shapes=[pltpu.VMEM((width, bn), BF16)] * N_BRANCHES,
        compiler_params=pltpu.CompilerParams(
            dimension_semantics=("arbitrary", "arbitrary"),
            vmem_limit_bytes=_vmem_limit(blocks + 3 * width * bn)),
        name="gated_merge",
    )(ua, ub, uc, pa, pb, pc, proj, proj, proj, b_merge, b_merge, b_merge)


def _out_kernel(y_ref, w_ref, x_ref, g_ref, o_ref, *, bn):
    j = pl.program_id(1)
    cols = pl.ds(pl.multiple_of(j * bn, bn), bn)
    o_ref[:, cols] = jnp.dot(y_ref[...], w_ref[...], preferred_element_type=F32)

    @pl.when(j == pl.num_programs(1) - 1)
    def _normalize():
        out = o_ref[...]
        inv = lax.rsqrt(jnp.mean(out * out, axis=-1, keepdims=True) + RMS_EPS)
        o_ref[...] = x_ref[...] + out * inv * g_ref[...]


def _out_proj(y, w_o, x2d, gain):
    m, d = y.shape
    bm = _tile(m, 512)
    bn = _tile(d, 512)
    blocks = bm * d * 2 + d * bn * 2 + 2 * bm * d * 4
    return pl.pallas_call(
        functools.partial(_out_kernel, bn=bn),
        grid=(m // bm, d // bn),
        in_specs=[pl.BlockSpec((bm, d), lambda i, j: (i, 0)),
                  pl.BlockSpec((d, bn), lambda i, j: (0, j)),
                  pl.BlockSpec((bm, d), lambda i, j: (i, 0)),
                  pl.BlockSpec((1, d), lambda i, j: (0, 0))],
        out_specs=pl.BlockSpec((bm, d), lambda i, j: (i, 0)),
        out_shape=jax.ShapeDtypeStruct((m, d), F32),
        compiler_params=pltpu.CompilerParams(
            dimension_semantics=("arbitrary", "arbitrary"),
            vmem_limit_bytes=_vmem_limit(blocks)),
        name="out_proj_norm_residual",
    )(y, w_o, x2d, gain.reshape(1, d))


def _layer(x2d, b, s, tables, layer, pre_g, post_g, w_in, b_merge, conv_w, p_a, p_b, p_c, w_o):
    d_model = x2d.shape[-1]
    width = conv_w.shape[-1]
    hw = width // HEAD_DIM

    h = _rmsnorm(x2d, pre_g)
    proj = _in_proj(h, w_in, layer, tables, 2 * width)

    ua = _moba(proj, b, s, width, 0 * hw, 1 * hw, 2 * hw, 3 * hw)
    ub = _conv_branch(proj, conv_w, b, s, width, 4, 5, 6, 7)
    uc = _stick(proj, b, s, width, 8 * hw, 9 * hw, 10 * hw, 11 * hw)

    bn = _tile(d_model, 512)
    y = _merge(ua, ub, uc, p_a, p_b, p_c, layer, proj, b_merge.reshape(1, -1), d_model,
               (12 * width) // bn)
    return _out_proj(y, w_o.astype(BF16), x2d, post_g)


def kernel(x, pre_norm_gain, post_norm_gain, w_in, b_merge_gate, conv_w,
           w_branch_a, w_branch_b, w_branch_c, w_out):
    b, s, d_model = x.shape
    depth = w_in.shape[0]
    assert s % MOBA_BLOCK == 0 and d_model % 128 == 0
    tables = _rope_tables(s)
    x2d = x.reshape(b * s, d_model)
    for layer in range(depth):
        x2d = _layer(x2d, b, s, tables, layer, pre_norm_gain[layer], post_norm_gain[layer],
                     w_in, b_merge_gate[layer], conv_w[layer], w_branch_a, w_branch_b,
                     w_branch_c, w_out[layer])
    return x2d.reshape(b, s, d_model)
```

```python
import functools
import math

import jax
import jax.numpy as jnp
from jax import lax
from jax.experimental import pallas as pl
from jax.experimental.pallas import tpu as pltpu

HEAD_DIM = 128
MOBA_BLOCK = 256
MOBA_TOPK = 3
ROPE_THETA = 500000.0
ROPE_DIM = HEAD_DIM // 4
CONV_WIDTH = 3
N_BRANCHES = 3
RMS_EPS = 1e-6

V7X_VMEM_BYTES = 64 * 1024 * 1024
VMEM_REQUEST_CAP = 60000 * 1024
SUBLANES = 8

F32 = jnp.float32
BF16 = jnp.bfloat16


def _vmem_limit(block_bytes):
    return int(min(VMEM_REQUEST_CAP, max(32 * 1024 * 1024, 2 * block_bytes + 8 * 1024 * 1024)))


def _tile(n, target):
    if n <= target:
        return n
    t = target - target % 128
    while t >= 128:
        if n % t == 0:
            return t
        t -= 128
    return n


def _sigmoid(x):
    return 1.0 / (1.0 + jnp.exp(-x))


def _silu(x):
    return x * _sigmoid(x)


def _rmsnorm_kernel(x_ref, g_ref, o_ref):
    x = x_ref[...]
    inv = lax.rsqrt(jnp.mean(x * x, axis=-1, keepdims=True) + RMS_EPS)
    o_ref[...] = (x * inv * g_ref[...]).astype(o_ref.dtype)


def _rmsnorm(x2d, gain):
    m, d = x2d.shape
    bm = _tile(m, 256)
    return pl.pallas_call(
        _rmsnorm_kernel,
        grid=(m // bm,),
        in_specs=[pl.BlockSpec((bm, d), lambda i: (i, 0)),
                  pl.BlockSpec((1, d), lambda i: (0, 0))],
        out_specs=pl.BlockSpec((bm, d), lambda i: (i, 0)),
        out_shape=jax.ShapeDtypeStruct((m, d), BF16),
        compiler_params=pltpu.CompilerParams(dimension_semantics=("arbitrary",)),
        name="rmsnorm_pre",
    )(x2d, gain.reshape(1, d))


def _rope_tables(s):
    half = ROPE_DIM // 2
    inv_freq = ROPE_THETA ** (-jnp.arange(half, dtype=F32) / half)
    ang = jnp.arange(s).astype(F32)[:, None] * inv_freq[None, :]
    cos = jnp.cos(ang)
    sin = jnp.sin(ang)
    rest = HEAD_DIM - ROPE_DIM
    zeros_h = jnp.zeros((s, half), F32)
    cos_t = jnp.concatenate([cos, cos, jnp.ones((s, rest), F32)], axis=-1)
    sin_up = jnp.concatenate([zeros_h, sin, jnp.zeros((s, rest), F32)], axis=-1)
    sin_dn = jnp.concatenate([-sin, zeros_h, jnp.zeros((s, rest), F32)], axis=-1)
    return cos_t, sin_up, sin_dn


def _rope(x, cos_t, sin_up, sin_dn):
    half = ROPE_DIM // 2
    return (x * cos_t + pltpu.roll(x, half, 1) * sin_up
            + pltpu.roll(x, HEAD_DIM - half, 1) * sin_dn)


def _in_proj_kernel(x_ref, w_ref, cos_ref, up_ref, dn_ref, *refs, n_rope_tiles, n_casts):
    cast_srcs, o_ref, cast_dsts = refs[:n_casts], refs[n_casts], refs[n_casts + 1:]
    for src, dst in zip(cast_srcs, cast_dsts):
        dst[...] = src[...].astype(BF16)

    o_ref[...] = jnp.dot(x_ref[...], w_ref[...], preferred_element_type=F32)

    @pl.when(pl.program_id(1) < n_rope_tiles)
    def _rotary():
        cos_t, sin_up, sin_dn = cos_ref[...], up_ref[...], dn_ref[...]
        for h in range(o_ref.shape[1] // HEAD_DIM):
            cols = slice(h * HEAD_DIM, (h + 1) * HEAD_DIM)
            o_ref[:, cols] = _rope(o_ref[:, cols], cos_t, sin_up, sin_dn)


BF16_ROW_TILE = 16


def _in_proj(h, w_bf16, tables, rope_cols, casts):
    m, k = h.shape
    _, n = w_bf16.shape
    s = tables[0].shape[0]
    bm = _tile(min(m, s), 1024)
    bn = _tile(math.gcd(n, rope_cols), 1024)
    assert s % bm == 0 and rope_cols % bn == 0 and n % bn == 0
    n_rope_tiles = rope_cols // bn
    pos_blocks = s // bm
    nj = n // bn
    steps = (m // bm) * nj

    table_spec = pl.BlockSpec((bm, HEAD_DIM), lambda i, j: (i % pos_blocks, 0))
    in_specs = [pl.BlockSpec((bm, k), lambda i, j: (i, 0)),
                pl.BlockSpec((k, bn), lambda i, j: (0, j)),
                table_spec, table_spec, table_spec]
    out_specs = [pl.BlockSpec((bm, bn), lambda i, j: (i, j))]
    out_shape = [jax.ShapeDtypeStruct((m, n), F32)]
    args = [h, w_bf16, *tables]
    for stack, layer in casts:
        _, r, wdt = stack.shape
        rows = -(-r // steps)
        rows = -(-rows // BF16_ROW_TILE) * BF16_ROW_TILE
        assert r % rows == 0
        last = r // rows - 1
        in_specs.append(pl.BlockSpec(
            (None, rows, wdt), lambda i, j, layer=layer, last=last:
            (layer, jnp.minimum(i * nj + j, last), 0)))
        out_specs.append(pl.BlockSpec(
            (rows, wdt), lambda i, j, last=last: (jnp.minimum(i * nj + j, last), 0)))
        out_shape.append(jax.ShapeDtypeStruct((r, wdt), BF16))
        args.append(stack)
    outs = pl.pallas_call(
        functools.partial(_in_proj_kernel, n_rope_tiles=n_rope_tiles, n_casts=len(casts)),
        grid=(m // bm, nj),
        in_specs=in_specs, out_specs=out_specs, out_shape=out_shape,
        compiler_params=pltpu.CompilerParams(
            dimension_semantics=("arbitrary", "arbitrary"),
            vmem_limit_bytes=VMEM_REQUEST_CAP),
        name="in_proj",
    )(*args)
    return outs[0], outs[1:]


_TRANS_B = (((1,), (1,)), ((), ()))


MOBA_HEADS_PER_STEP = 2
STICK_HEADS_PER_STEP = 4
MOBA_CHUNK_BLOCKS = 4
LOG2_E = 1.4426950408889634


def _head_cols(g):
    return slice(g * HEAD_DIM, (g + 1) * HEAD_DIM)


def _moba_kernel(q_ref, k_ref, v_ref, z_ref, o_ref, kr_scr, vt_blk_scr, vt_chunk_scr, kmean_scr,
                 bias_scr, s0_scr, s1_scr, *, n_blk, tq, n_grp, n_cb):
    i = pl.program_id(2)
    blk = MOBA_BLOCK
    chunk = n_cb * blk
    scale2 = HEAD_DIM ** -0.5 * LOG2_E
    neg_inf = -jnp.inf
    heads = range(n_grp)

    @pl.when(i == 0)
    def _prepare_keys():
        def one_chunk(c, carry):
            for u in range(n_cb):
                j = c * n_cb + u
                rows = pl.ds(pl.multiple_of(j * blk, blk), blk)
                for g in heads:
                    kr = k_ref[rows, _head_cols(g)]
                    kr_scr[g, rows, :] = kr.astype(BF16)
                    vt = v_ref[rows, _head_cols(g)].T.astype(BF16)
                    vt_blk_scr[g, j] = vt
                    vt_chunk_scr[g, c, :, u * blk:(u + 1) * blk] = vt
                    kmean_scr[g, pl.ds(j, 1), :] = jnp.mean(kr, axis=0, keepdims=True)
            return carry
        lax.fori_loop(0, n_blk // n_cb, one_chunk, 0)

    own = i
    own_rows = pl.ds(pl.multiple_of(own * blk, blk), blk)
    blk_id = lax.broadcasted_iota(jnp.int32, (n_blk, tq), 0).astype(F32)
    past = blk_id < own.astype(F32)
    causal = (lax.broadcasted_iota(jnp.int32, (blk, tq), 0)
              <= lax.broadcasted_iota(jnp.int32, (blk, tq), 1))

    qs = [q_ref[:, _head_cols(g)] for g in heads]
    qbs = [(q * scale2).astype(BF16) for q in qs]
    ss = [lax.dot_general(kr_scr[g, own_rows, :], qbs[g], _TRANS_B, preferred_element_type=F32)
          for g in heads]
    gates = [lax.dot_general(kmean_scr[g], qs[g], _TRANS_B, precision=lax.Precision.HIGHEST,
                             preferred_element_type=F32) for g in heads]
    gscs = [jnp.where(past, gate, neg_inf) for gate in gates]
    sels = [jnp.zeros((n_blk, tq), jnp.bool_) for _ in heads]
    for _ in range(min(MOBA_TOPK, n_blk)):
        gmaxs = [jnp.max(gsc, axis=0, keepdims=True) for gsc in gscs]
        firsts = [jnp.min(jnp.where(gscs[g] == gmaxs[g], blk_id, float(n_blk)), axis=0,
                          keepdims=True) for g in heads]
        picks = [blk_id == first for first in firsts]
        sels = [jnp.logical_or(sels[g], picks[g]) for g in heads]
        gscs = [jnp.where(picks[g], neg_inf, gscs[g]) for g in heads]
    for g in heads:
        bias_scr[g] = jnp.where(jnp.logical_and(sels[g], past), 0.0, neg_inf)

    ss = [jnp.where(causal, s, neg_inf) for s in ss]
    ms = [jnp.max(s, axis=0, keepdims=True) for s in ss]
    ps = [jnp.exp2(ss[g] - ms[g]) for g in heads]
    ls = [jnp.sum(p, axis=0, keepdims=True) for p in ps]
    accs = [jnp.dot(vt_blk_scr[g, own], ps[g].astype(BF16), preferred_element_type=F32)
            for g in heads]

    n_chunks = (own + n_cb - 1) // n_cb
    last_chunk = n_blk // n_cb - 1

    def score_chunk(c, dst_scr):
        c = jnp.minimum(c, last_chunk)
        rows = pl.ds(pl.multiple_of(c * chunk, chunk), chunk)
        for g in heads:
            dst_scr[g] = lax.dot_general(kr_scr[g, rows, :], qbs[g], _TRANS_B,
                                         preferred_element_type=F32)

    def absorb_chunk(c, src_scr, state):
        ms, ls, accs = state
        m_news, new_ls, new_accs = [], [], []
        for g in heads:
            s = jnp.concatenate(
                [src_scr[g, u * blk:(u + 1) * blk, :] + bias_scr[g, pl.ds(c * n_cb + u, 1), :]
                 for u in range(n_cb)], axis=0)
            m_new = jnp.maximum(ms[g], jnp.max(s, axis=0, keepdims=True))
            alpha = jnp.exp2(ms[g] - m_new)
            p = jnp.exp2(s - m_new)
            new_ls.append(alpha * ls[g] + jnp.sum(p, axis=0, keepdims=True))
            new_accs.append(alpha * accs[g] + jnp.dot(vt_chunk_scr[g, c], p.astype(BF16),
                                                      preferred_element_type=F32))
            m_news.append(m_new)
        return tuple(m_news), tuple(new_ls), tuple(new_accs)

    score_chunk(0, s0_scr)

    def chunk_pair(p, state):
        c = 2 * p
        score_chunk(c + 1, s1_scr)
        state = absorb_chunk(c, s0_scr, state)

        def second(state):
            score_chunk(c + 2, s0_scr)
            return absorb_chunk(c + 1, s1_scr, state)

        return lax.cond(c + 1 < n_chunks, second, lambda st: st, state)

    _, ls, accs = lax.fori_loop(0, (n_chunks + 1) // 2, chunk_pair,
                                (tuple(ms), tuple(ls), tuple(accs)))
    for g in heads:
        out = (accs[g] * (1.0 / ls[g])).T
        o_ref[:, _head_cols(g)] = (out * _silu(z_ref[:, _head_cols(g)])).astype(o_ref.dtype)


def _moba(proj, b, s, width, col_q, col_k, col_v, col_z):
    n_grp = MOBA_HEADS_PER_STEP
    n_steps = width // (HEAD_DIM * n_grp)
    n_blk = s // MOBA_BLOCK
    n_cb = MOBA_CHUNK_BLOCKS
    assert n_blk % n_cb == 0
    tq = MOBA_BLOCK
    nq = s // tq
    d = HEAD_DIM
    dg = d * n_grp
    assert col_q % n_grp == 0 and col_k % n_grp == 0 and col_v % n_grp == 0 and col_z % n_grp == 0

    q_spec = lambda off: pl.BlockSpec((tq, dg), lambda bi, h, i: (bi * nq + i, off // n_grp + h))
    kv_spec = lambda off: pl.BlockSpec((s, dg), lambda bi, h, i: (bi, off // n_grp + h))
    blocks = 2 * s * dg * 4 + 2 * tq * dg * 4 + tq * dg * 2 + 3 * s * dg * 2
    return pl.pallas_call(
        functools.partial(_moba_kernel, n_blk=n_blk, tq=tq, n_grp=n_grp, n_cb=n_cb),
        grid=(b, n_steps, nq),
        in_specs=[q_spec(col_q), kv_spec(col_k), kv_spec(col_v), q_spec(col_z)],
        out_specs=pl.BlockSpec((tq, dg), lambda bi, h, i: (bi * nq + i, h)),
        out_shape=jax.ShapeDtypeStruct((b * s, width), BF16),
        scratch_shapes=[pltpu.VMEM((n_grp, s, d), BF16),
                        pltpu.VMEM((n_grp, n_blk, d, MOBA_BLOCK), BF16),
                        pltpu.VMEM((n_grp, n_blk // n_cb, d, n_cb * MOBA_BLOCK), BF16),
                        pltpu.VMEM((n_grp, n_blk, d), F32),
                        pltpu.VMEM((n_grp, n_blk, tq), F32),
                        pltpu.VMEM((n_grp, n_cb * MOBA_BLOCK, tq), F32),
                        pltpu.VMEM((n_grp, n_cb * MOBA_BLOCK, tq), F32)],
        compiler_params=pltpu.CompilerParams(
            dimension_semantics=("arbitrary", "arbitrary", "arbitrary"),
            vmem_limit_bytes=_vmem_limit(blocks + n_grp * n_cb * MOBA_BLOCK * tq * 4)),
        name="moba_attention",
    )(proj, proj, proj, proj)


F32_EXP2_ZERO_BELOW = -152.0


def _softplus2(z2):
    return jnp.maximum(z2, 0.0) + jnp.log2(1.0 + jnp.exp2(-jnp.abs(z2)))


def _suffix_sums(tri, x):
    hi = x.astype(BF16)
    lo = (x - hi.astype(F32)).astype(BF16)
    return (jnp.dot(tri, hi, preferred_element_type=F32)
            + jnp.dot(tri, lo, preferred_element_type=F32))


def _stick_kernel(q_ref, k_ref, v_ref, z_ref, tri_ref, o_ref, kb_scr, vt_scr, *, tq, n_grp):
    i = pl.program_id(2)
    scale2 = HEAD_DIM ** -0.5 * LOG2_E
    n_tiles = k_ref.shape[0] // tq
    heads = range(n_grp)

    @pl.when(i == 0)
    def _prepare_keys():
        def one_tile(j, carry):
            rows = pl.ds(pl.multiple_of(j * tq, tq), tq)
            for g in heads:
                kb_scr[g, rows, :] = k_ref[rows, _head_cols(g)].astype(BF16)
                vt_scr[g, j] = v_ref[rows, _head_cols(g)].T.astype(BF16)
            return carry
        lax.fori_loop(0, n_tiles, one_tile, 0)

    tri = tri_ref[...]
    qbs = [(q_ref[:, _head_cols(g)] * scale2).astype(BF16) for g in heads]
    causal = (lax.broadcasted_iota(jnp.int32, (tq, tq), 0)
              < lax.broadcasted_iota(jnp.int32, (tq, tq), 1))

    rows = pl.ds(pl.multiple_of(i * tq, tq), tq)
    zs = [lax.dot_general(kb_scr[g, rows, :], qbs[g], _TRANS_B, preferred_element_type=F32)
          for g in heads]
    sps = [_softplus2(z) for z in zs]
    drops = [jnp.where(causal, sp, 0.0) for sp in sps]
    removed = [jnp.sum(dr, axis=0, keepdims=True) for dr in drops]
    laters = [_suffix_sums(tri, dr) for dr in drops]
    a = [jnp.where(causal, jnp.exp2(zs[g] - sps[g] - laters[g]), 0.0).astype(BF16) for g in heads]
    accs = [jnp.dot(vt_scr[g, i], a[g], preferred_element_type=F32) for g in heads]

    def still_needed(removed):
        least = removed[0]
        for r in removed[1:]:
            least = jnp.minimum(least, r)
        return (-jnp.min(least) > F32_EXP2_ZERO_BELOW).astype(jnp.int32)

    def more_tiles(state):
        t, go = state[0], state[1]
        return jnp.logical_and(t < i, go > 0)

    def earlier_tile(state):
        t, _, removed, accs = state
        j = i - 1 - t
        rows = pl.ds(pl.multiple_of(j * tq, tq), tq)
        zs = [lax.dot_general(kb_scr[g, rows, :], qbs[g], _TRANS_B, preferred_element_type=F32)
              for g in heads]
        sps = [_softplus2(z) for z in zs]
        new_removed = [removed[g] + jnp.sum(sps[g], axis=0, keepdims=True) for g in heads]
        laters = [_suffix_sums(tri, sps[g]) + removed[g] for g in heads]
        a = [jnp.exp2(zs[g] - sps[g] - laters[g]).astype(BF16) for g in heads]
        new_accs = [accs[g] + jnp.dot(vt_scr[g, j], a[g], preferred_element_type=F32)
                    for g in heads]
        return (t + 1, still_needed(new_removed), tuple(new_removed), tuple(new_accs))

    state = lax.while_loop(more_tiles, earlier_tile,
                           (jnp.int32(0), jnp.int32(1), tuple(removed), tuple(accs)))
    for g in heads:
        o_ref[:, _head_cols(g)] = (state[3][g].T * _silu(z_ref[:, _head_cols(g)])).astype(o_ref.dtype)


def _stick(proj, b, s, width, col_q, col_k, col_v, col_z):
    n_grp = math.gcd(STICK_HEADS_PER_STEP, width // HEAD_DIM)
    n_steps = width // (HEAD_DIM * n_grp)
    tq = 256
    assert s % tq == 0
    nq = s // tq
    d = HEAD_DIM
    dg = d * n_grp
    assert col_q % n_grp == 0 and col_k % n_grp == 0 and col_v % n_grp == 0 and col_z % n_grp == 0
    tri = (jnp.arange(tq)[None, :] > jnp.arange(tq)[:, None]).astype(BF16)

    q_spec = lambda off: pl.BlockSpec((tq, dg), lambda bi, h, i: (bi * nq + i, off // n_grp + h))
    kv_spec = lambda off: pl.BlockSpec((s, dg), lambda bi, h, i: (bi, off // n_grp + h))
    blocks = 2 * s * dg * 4 + 3 * tq * dg * 4 + tq * tq * 2 + 2 * s * dg * 2
    return pl.pallas_call(
        functools.partial(_stick_kernel, tq=tq, n_grp=n_grp),
        grid=(b, n_steps, nq),
        in_specs=[q_spec(col_q), kv_spec(col_k), kv_spec(col_v), q_spec(col_z),
                  pl.BlockSpec((tq, tq), lambda bi, h, i: (0, 0))],
        out_specs=pl.BlockSpec((tq, dg), lambda bi, h, i: (bi * nq + i, h)),
        out_shape=jax.ShapeDtypeStruct((b * s, width), BF16),
        scratch_shapes=[pltpu.VMEM((n_grp, s, d), BF16),
                        pltpu.VMEM((n_grp, s // tq, d, tq), BF16)],
        compiler_params=pltpu.CompilerParams(
            dimension_semantics=("arbitrary", "arbitrary", "arbitrary"),
            vmem_limit_bytes=_vmem_limit(blocks)),
        name="stick_breaking_attention",
    )(proj, proj, proj, proj, tri)


def _conv_kernel(bb_ref, bc_ref, bx_ref, bz_ref, hc_ref, hx_ref, w_ref, o_ref):
    i = pl.program_id(1)
    xc = bc_ref[...] * bx_ref[...]
    halo = jnp.where(i == 0, 0.0, hc_ref[...] * hx_ref[...])
    full = jnp.concatenate([halo, xc], axis=0)
    prev1 = pltpu.roll(full, 1, 0)[SUBLANES:]
    prev2 = pltpu.roll(full, 2, 0)[SUBLANES:]
    w = w_ref[...]
    y = w[0:1, :] * prev2 + w[1:2, :] * prev1 + w[2:3, :] * xc
    o_ref[...] = (bb_ref[...] * y * _silu(bz_ref[...])).astype(o_ref.dtype)


def _conv_branch(proj, conv_w, b, s, width, col_bb, col_bc, col_bx, col_bz):
    ts = _tile(s, 512)
    tw = _tile(width, 512)
    ns = s // ts
    nw = width // tw
    halo_blocks = ts // SUBLANES

    main = lambda off: pl.BlockSpec((ts, tw), lambda bi, i, j: (bi * ns + i, off * nw + j))
    halo = lambda off: pl.BlockSpec(
        (SUBLANES, tw),
        lambda bi, i, j: (jnp.maximum((bi * ns + i) * halo_blocks - 1, 0), off * nw + j))
    return pl.pallas_call(
        _conv_kernel,
        grid=(b, ns, nw),
        in_specs=[main(col_bb), main(col_bc), main(col_bx), main(col_bz),
                  halo(col_bc), halo(col_bx),
                  pl.BlockSpec((CONV_WIDTH, tw), lambda bi, i, j: (0, j))],
        out_specs=pl.BlockSpec((ts, tw), lambda bi, i, j: (bi * ns + i, j)),
        out_shape=jax.ShapeDtypeStruct((b * s, width), BF16),
        compiler_params=pltpu.CompilerParams(
            dimension_semantics=("arbitrary", "arbitrary", "arbitrary")),
        name="short_conv_branch",
    )(proj, proj, proj, proj, proj, proj, conv_w)


def _merge_kernel(ua_ref, ub_ref, uc_ref, pa_ref, pb_ref, pc_ref,
                  ga_ref, gb_ref, gc_ref, ba_ref, bb_ref, bc_ref, o_ref,
                  pa_scr, pb_scr, pc_scr):
    @pl.when(pl.program_id(1) == 0)
    def _cast_weights():
        pa_scr[...] = pa_ref[...].astype(BF16)
        pb_scr[...] = pb_ref[...].astype(BF16)
        pc_scr[...] = pc_ref[...].astype(BF16)

    def branch(u_ref, p_scr, g_ref, b_ref):
        gate = _sigmoid(g_ref[...] + b_ref[...])
        return gate * jnp.dot(u_ref[...], p_scr[...], preferred_element_type=F32)
    y = (branch(ua_ref, pa_scr, ga_ref, ba_ref) + branch(ub_ref, pb_scr, gb_ref, bb_ref)
         + branch(uc_ref, pc_scr, gc_ref, bc_ref))
    o_ref[...] = y.astype(o_ref.dtype)


def _merge(ua, ub, uc, pa, pb, pc, layer, proj, b_merge, d_model, col_g):
    m, width = ua.shape
    bm = _tile(m, 512)
    bn = _tile(d_model, 512)
    nn = d_model // bn

    u_spec = pl.BlockSpec((bm, width), lambda j, i: (i, 0))
    p_spec = pl.BlockSpec((None, width, bn), lambda j, i: (layer, 0, j))
    g_spec = lambda br: pl.BlockSpec((bm, bn), lambda j, i: (i, col_g + br * nn + j))
    b_spec = lambda br: pl.BlockSpec((1, bn), lambda j, i: (0, br * nn + j))
    blocks = 3 * bm * width * 2 + 3 * width * bn * 4 + 3 * bm * bn * 4 + bm * bn * 2
    return pl.pallas_call(
        _merge_kernel,
        grid=(nn, m // bm),
        in_specs=[u_spec, u_spec, u_spec, p_spec, p_spec, p_spec,
                  g_spec(0), g_spec(1), g_spec(2), b_spec(0), b_spec(1), b_spec(2)],
        out_specs=pl.BlockSpec((bm, bn), lambda j, i: (i, j)),
        out_shape=jax.ShapeDtypeStruct((m, d_model), BF16),
        scratch_shapes=[pltpu.VMEM((width, bn), BF16)] * N_BRANCHES,
        compiler_params=pltpu.CompilerParams(
            dimension_semantics=("arbitrary", "arbitrary"),
            vmem_limit_bytes=_vmem_limit(blocks + 3 * width * bn)),
        name="gated_merge",
    )(ua, ub, uc, pa, pb, pc, proj, proj, proj, b_merge, b_merge, b_merge)


def _out_kernel(y_ref, w_ref, x_ref, g_ref, o_ref, *, bn):
    j = pl.program_id(1)
    cols = pl.ds(pl.multiple_of(j * bn, bn), bn)
    o_ref[:, cols] = jnp.dot(y_ref[...], w_ref[...], preferred_element_type=F32)

    @pl.when(j == pl.num_programs(1) - 1)
    def _normalize():
        out = o_ref[...]
        inv = lax.rsqrt(jnp.mean(out * out, axis=-1, keepdims=True) + RMS_EPS)
        o_ref[...] = x_ref[...] + out * inv * g_ref[...]


def _out_proj(y, w_o, x2d, gain):
    m, d = y.shape
    bm = _tile(m, 512)
    bn = _tile(d, 512)
    blocks = bm * d * 2 + d * bn * 2 + 2 * bm * d * 4
    return pl.pallas_call(
        functools.partial(_out_kernel, bn=bn),
        grid=(m // bm, d // bn),
        in_specs=[pl.BlockSpec((bm, d), lambda i, j: (i, 0)),
                  pl.BlockSpec((d, bn), lambda i, j: (0, j)),
                  pl.BlockSpec((bm, d), lambda i, j: (i, 0)),
                  pl.BlockSpec((1, d), lambda i, j: (0, 0))],
        out_specs=pl.BlockSpec((bm, d), lambda i, j: (i, 0)),
        out_shape=jax.ShapeDtypeStruct((m, d), F32),
        compiler_params=pltpu.CompilerParams(
            dimension_semantics=("arbitrary", "arbitrary"),
            vmem_limit_bytes=_vmem_limit(blocks)),
        name="out_proj_norm_residual",
    )(y, w_o, x2d, gain.reshape(1, d))


def kernel(x, pre_norm_gain, post_norm_gain, w_in, b_merge_gate, conv_w,
           w_branch_a, w_branch_b, w_branch_c, w_out):
    b, s, d_model = x.shape
    depth, _, n_cols = w_in.shape
    width = conv_w.shape[-1]
    hw = width // HEAD_DIM
    assert s % MOBA_BLOCK == 0 and d_model % 128 == 0
    tables = _rope_tables(s)
    x2d = x.reshape(b * s, d_model)
    w_in_rows = w_in.reshape(depth, n_cols, d_model)
    w_in_bf16 = w_in[0].astype(BF16)
    bn = _tile(d_model, 512)

    for layer in range(depth):
        h = _rmsnorm(x2d, pre_norm_gain[layer])
        casts = [(w_out, layer)]
        if layer + 1 < depth:
            casts.append((w_in_rows, layer + 1))
        proj, cast_outs = _in_proj(h, w_in_bf16, tables, 2 * width, casts)
        w_o = cast_outs[0]
        if layer + 1 < depth:
            w_in_bf16 = cast_outs[1].reshape(d_model, n_cols)

        ua = _moba(proj, b, s, width, 0 * hw, 1 * hw, 2 * hw, 3 * hw)
        ub = _conv_branch(proj, conv_w[layer], b, s, width, 4, 5, 6, 7)
        uc = _stick(proj, b, s, width, 8 * hw, 9 * hw, 10 * hw, 11 * hw)
        y = _merge(ua, ub, uc, w_branch_a, w_branch_b, w_branch_c, layer, proj,
                   b_merge_gate[layer].reshape(1, -1), d_model, (12 * width) // bn)
        x2d = _out_proj(y, w_o, x2d, post_norm_gain[layer])
    return x2d.reshape(b, s, d_model)
```

```python
import functools
import math

import jax
import jax.numpy as jnp
from jax import lax
from jax.experimental import pallas as pl
from jax.experimental.pallas import tpu as pltpu

HEAD_DIM = 128
MOBA_BLOCK = 256
MOBA_TOPK = 3
ROPE_THETA = 500000.0
ROPE_DIM = HEAD_DIM // 4
CONV_WIDTH = 3
N_BRANCHES = 3
RMS_EPS = 1e-6

V7X_VMEM_BYTES = 64 * 1024 * 1024
VMEM_REQUEST_CAP = 60000 * 1024
SUBLANES = 8

F32 = jnp.float32
BF16 = jnp.bfloat16


def _vmem_limit(block_bytes):
    return int(min(VMEM_REQUEST_CAP, max(32 * 1024 * 1024, 2 * block_bytes + 8 * 1024 * 1024)))


def _tile(n, target):
    if n <= target:
        return n
    t = target - target % 128
    while t >= 128:
        if n % t == 0:
            return t
        t -= 128
    return n


def _sigmoid(x):
    return 1.0 / (1.0 + jnp.exp(-x))


def _silu(x):
    return x * _sigmoid(x)


def _rmsnorm_kernel(x_ref, g_ref, o_ref):
    x = x_ref[...]
    inv = lax.rsqrt(jnp.mean(x * x, axis=-1, keepdims=True) + RMS_EPS)
    o_ref[...] = (x * inv * g_ref[...]).astype(o_ref.dtype)


def _rmsnorm(x2d, gain):
    m, d = x2d.shape
    bm = _tile(m, 256)
    return pl.pallas_call(
        _rmsnorm_kernel,
        grid=(m // bm,),
        in_specs=[pl.BlockSpec((bm, d), lambda i: (i, 0)),
                  pl.BlockSpec((1, d), lambda i: (0, 0))],
        out_specs=pl.BlockSpec((bm, d), lambda i: (i, 0)),
        out_shape=jax.ShapeDtypeStruct((m, d), BF16),
        compiler_params=pltpu.CompilerParams(dimension_semantics=("arbitrary",)),
        name="rmsnorm_pre",
    )(x2d, gain.reshape(1, d))


def _rope_tables(s):
    half = ROPE_DIM // 2
    inv_freq = ROPE_THETA ** (-jnp.arange(half, dtype=F32) / half)
    ang = jnp.arange(s).astype(F32)[:, None] * inv_freq[None, :]
    cos = jnp.cos(ang)
    sin = jnp.sin(ang)
    rest = HEAD_DIM - ROPE_DIM
    zeros_h = jnp.zeros((s, half), F32)
    cos_t = jnp.concatenate([cos, cos, jnp.ones((s, rest), F32)], axis=-1)
    sin_up = jnp.concatenate([zeros_h, sin, jnp.zeros((s, rest), F32)], axis=-1)
    sin_dn = jnp.concatenate([-sin, zeros_h, jnp.zeros((s, rest), F32)], axis=-1)
    return cos_t, sin_up, sin_dn


def _rope(x, cos_t, sin_up, sin_dn):
    half = ROPE_DIM // 2
    return (x * cos_t + pltpu.roll(x, half, 1) * sin_up
            + pltpu.roll(x, HEAD_DIM - half, 1) * sin_dn)


def _in_proj_kernel(x_ref, w_ref, cos_ref, up_ref, dn_ref, *refs, n_rope_tiles, n_casts):
    cast_srcs, o_ref, cast_dsts = refs[:n_casts], refs[n_casts], refs[n_casts + 1:]
    for src, dst in zip(cast_srcs, cast_dsts):
        dst[...] = src[...].astype(BF16)

    o_ref[...] = jnp.dot(x_ref[...], w_ref[...], preferred_element_type=F32)

    @pl.when(pl.program_id(1) < n_rope_tiles)
    def _rotary():
        cos_t, sin_up, sin_dn = cos_ref[...], up_ref[...], dn_ref[...]
        for h in range(o_ref.shape[1] // HEAD_DIM):
            cols = slice(h * HEAD_DIM, (h + 1) * HEAD_DIM)
            o_ref[:, cols] = _rope(o_ref[:, cols], cos_t, sin_up, sin_dn)


BF16_ROW_TILE = 16
LANES = 128


def _in_proj(h, w_bf16, tables, rope_cols, casts):
    m, k = h.shape
    _, n = w_bf16.shape
    s = tables[0].shape[0]
    bm = _tile(min(m, s), 1024)
    bn = _tile(math.gcd(n, rope_cols), 1024)
    assert s % bm == 0 and rope_cols % bn == 0 and n % bn == 0
    n_rope_tiles = rope_cols // bn
    pos_blocks = s // bm
    nj = n // bn
    steps = (m // bm) * nj

    table_spec = pl.BlockSpec((bm, HEAD_DIM), lambda i, j: (i % pos_blocks, 0))
    in_specs = [pl.BlockSpec((bm, k), lambda i, j: (i, 0)),
                pl.BlockSpec((k, bn), lambda i, j: (0, j)),
                table_spec, table_spec, table_spec]
    out_specs = [pl.BlockSpec((bm, bn), lambda i, j: (i, j))]
    out_shape = [jax.ShapeDtypeStruct((m, n), F32)]
    args = [h, w_bf16, *tables]
    for stack, layer in casts:
        _, r, wdt = stack.shape
        if wdt >= steps * LANES:
            cols = -(-wdt // steps)
            cols = -(-cols // LANES) * LANES
            assert wdt % cols == 0
            last = wdt // cols - 1
            in_specs.append(pl.BlockSpec(
                (None, r, cols), lambda i, j, layer=layer, last=last:
                (layer, 0, jnp.minimum(i * nj + j, last))))
            out_specs.append(pl.BlockSpec(
                (r, cols), lambda i, j, last=last: (0, jnp.minimum(i * nj + j, last))))
        else:
            rows = -(-r // steps)
            rows = -(-rows // BF16_ROW_TILE) * BF16_ROW_TILE
            assert r % rows == 0
            last = r // rows - 1
            in_specs.append(pl.BlockSpec(
                (None, rows, wdt), lambda i, j, layer=layer, last=last:
                (layer, jnp.minimum(i * nj + j, last), 0)))
            out_specs.append(pl.BlockSpec(
                (rows, wdt), lambda i, j, last=last: (jnp.minimum(i * nj + j, last), 0)))
        out_shape.append(jax.ShapeDtypeStruct((r, wdt), BF16))
        args.append(stack)
    outs = pl.pallas_call(
        functools.partial(_in_proj_kernel, n_rope_tiles=n_rope_tiles, n_casts=len(casts)),
        grid=(m // bm, nj),
        in_specs=in_specs, out_specs=out_specs, out_shape=out_shape,
        compiler_params=pltpu.CompilerParams(
            dimension_semantics=("arbitrary", "arbitrary"),
            vmem_limit_bytes=VMEM_REQUEST_CAP),
        name="in_proj",
    )(*args)
    return outs[0], outs[1:]


_TRANS_B = (((1,), (1,)), ((), ()))


MOBA_HEADS_PER_STEP = 2
STICK_HEADS_PER_STEP = 4
MOBA_CHUNK_BLOCKS = 4
LOG2_E = 1.4426950408889634


def _head_cols(g):
    return slice(g * HEAD_DIM, (g + 1) * HEAD_DIM)


def _moba_kernel(q_ref, k_ref, v_ref, z_ref, o_ref, kr_scr, vt_blk_scr, vt_chunk_scr, kmean_scr,
                 bias_scr, s0_scr, s1_scr, *, n_blk, tq, n_grp, n_cb):
    i = pl.program_id(2)
    blk = MOBA_BLOCK
    chunk = n_cb * blk
    scale2 = HEAD_DIM ** -0.5 * LOG2_E
    neg_inf = -jnp.inf
    heads = range(n_grp)

    @pl.when(i == 0)
    def _prepare_keys():
        def one_chunk(c, carry):
            for u in range(n_cb):
                j = c * n_cb + u
                rows = pl.ds(pl.multiple_of(j * blk, blk), blk)
                for g in heads:
                    kr = k_ref[rows, _head_cols(g)]
                    kr_scr[g, rows, :] = kr.astype(BF16)
                    vt = v_ref[rows, _head_cols(g)].T.astype(BF16)
                    vt_blk_scr[g, j] = vt
                    vt_chunk_scr[g, c, :, u * blk:(u + 1) * blk] = vt
                    kmean_scr[g, pl.ds(j, 1), :] = jnp.mean(kr, axis=0, keepdims=True)
            return carry
        lax.fori_loop(0, n_blk // n_cb, one_chunk, 0)

    own = i
    own_rows = pl.ds(pl.multiple_of(own * blk, blk), blk)
    blk_id = lax.broadcasted_iota(jnp.int32, (n_blk, tq), 0).astype(F32)
    past = blk_id < own.astype(F32)
    causal = (lax.broadcasted_iota(jnp.int32, (blk, tq), 0)
              <= lax.broadcasted_iota(jnp.int32, (blk, tq), 1))

    qs = [q_ref[:, _head_cols(g)] for g in heads]
    qbs = [(q * scale2).astype(BF16) for q in qs]
    ss = [lax.dot_general(kr_scr[g, own_rows, :], qbs[g], _TRANS_B, preferred_element_type=F32)
          for g in heads]
    gates = [lax.dot_general(kmean_scr[g], qs[g], _TRANS_B, precision=lax.Precision.HIGHEST,
                             preferred_element_type=F32) for g in heads]
    gscs = [jnp.where(past, gate, neg_inf) for gate in gates]
    sels = [jnp.zeros((n_blk, tq), jnp.bool_) for _ in heads]
    for _ in range(min(MOBA_TOPK, n_blk)):
        gmaxs = [jnp.max(gsc, axis=0, keepdims=True) for gsc in gscs]
        firsts = [jnp.min(jnp.where(gscs[g] == gmaxs[g], blk_id, float(n_blk)), axis=0,
                          keepdims=True) for g in heads]
        picks = [blk_id == first for first in firsts]
        sels = [jnp.logical_or(sels[g], picks[g]) for g in heads]
        gscs = [jnp.where(picks[g], neg_inf, gscs[g]) for g in heads]
    for g in heads:
        bias_scr[g] = jnp.where(jnp.logical_and(sels[g], past), 0.0, neg_inf)

    ss = [jnp.where(causal, s, neg_inf) for s in ss]
    ms = [jnp.max(s, axis=0, keepdims=True) for s in ss]
    ps = [jnp.exp2(ss[g] - ms[g]) for g in heads]
    ls = [jnp.sum(p, axis=0, keepdims=True) for p in ps]
    accs = [jnp.dot(vt_blk_scr[g, own], ps[g].astype(BF16), preferred_element_type=F32)
            for g in heads]

    n_chunks = (own + n_cb - 1) // n_cb
    last_chunk = n_blk // n_cb - 1

    def score_chunk(c, dst_scr):
        c = jnp.minimum(c, last_chunk)
        rows = pl.ds(pl.multiple_of(c * chunk, chunk), chunk)
        for g in heads:
            dst_scr[g] = lax.dot_general(kr_scr[g, rows, :], qbs[g], _TRANS_B,
                                         preferred_element_type=F32)

    def absorb_chunk(c, src_scr, state):
        ms, ls, accs = state
        m_news, new_ls, new_accs = [], [], []
        for g in heads:
            s = jnp.concatenate(
                [src_scr[g, u * blk:(u + 1) * blk, :] + bias_scr[g, pl.ds(c * n_cb + u, 1), :]
                 for u in range(n_cb)], axis=0)
            m_new = jnp.maximum(ms[g], jnp.max(s, axis=0, keepdims=True))
            alpha = jnp.exp2(ms[g] - m_new)
            p = jnp.exp2(s - m_new)
            new_ls.append(alpha * ls[g] + jnp.sum(p, axis=0, keepdims=True))
            new_accs.append(alpha * accs[g] + jnp.dot(vt_chunk_scr[g, c], p.astype(BF16),
                                                      preferred_element_type=F32))
            m_news.append(m_new)
        return tuple(m_news), tuple(new_ls), tuple(new_accs)

    score_chunk(0, s0_scr)

    def chunk_pair(p, state):
        c = 2 * p
        score_chunk(c + 1, s1_scr)
        state = absorb_chunk(c, s0_scr, state)

        def second(state):
            score_chunk(c + 2, s0_scr)
            return absorb_chunk(c + 1, s1_scr, state)

        return lax.cond(c + 1 < n_chunks, second, lambda st: st, state)

    _, ls, accs = lax.fori_loop(0, (n_chunks + 1) // 2, chunk_pair,
                                (tuple(ms), tuple(ls), tuple(accs)))
    for g in heads:
        out = (accs[g] * (1.0 / ls[g])).T
        o_ref[:, _head_cols(g)] = (out * _silu(z_ref[:, _head_cols(g)])).astype(o_ref.dtype)


def _moba(proj, b, s, width, col_q, col_k, col_v, col_z):
    n_grp = MOBA_HEADS_PER_STEP
    n_steps = width // (HEAD_DIM * n_grp)
    n_blk = s // MOBA_BLOCK
    n_cb = MOBA_CHUNK_BLOCKS
    assert n_blk % n_cb == 0
    tq = MOBA_BLOCK
    nq = s // tq
    d = HEAD_DIM
    dg = d * n_grp
    assert col_q % n_grp == 0 and col_k % n_grp == 0 and col_v % n_grp == 0 and col_z % n_grp == 0

    q_spec = lambda off: pl.BlockSpec((tq, dg), lambda bi, h, i: (bi * nq + i, off // n_grp + h))
    kv_spec = lambda off: pl.BlockSpec((s, dg), lambda bi, h, i: (bi, off // n_grp + h))
    blocks = 2 * s * dg * 4 + 2 * tq * dg * 4 + tq * dg * 2 + 3 * s * dg * 2
    return pl.pallas_call(
        functools.partial(_moba_kernel, n_blk=n_blk, tq=tq, n_grp=n_grp, n_cb=n_cb),
        grid=(b, n_steps, nq),
        in_specs=[q_spec(col_q), kv_spec(col_k), kv_spec(col_v), q_spec(col_z)],
        out_specs=pl.BlockSpec((tq, dg), lambda bi, h, i: (bi * nq + i, h)),
        out_shape=jax.ShapeDtypeStruct((b * s, width), BF16),
        scratch_shapes=[pltpu.VMEM((n_grp, s, d), BF16),
                        pltpu.VMEM((n_grp, n_blk, d, MOBA_BLOCK), BF16),
                        pltpu.VMEM((n_grp, n_blk // n_cb, d, n_cb * MOBA_BLOCK), BF16),
                        pltpu.VMEM((n_grp, n_blk, d), F32),
                        pltpu.VMEM((n_grp, n_blk, tq), F32),
                        pltpu.VMEM((n_grp, n_cb * MOBA_BLOCK, tq), F32),
                        pltpu.VMEM((n_grp, n_cb * MOBA_BLOCK, tq), F32)],
        compiler_params=pltpu.CompilerParams(
            dimension_semantics=("arbitrary", "arbitrary", "arbitrary"),
            vmem_limit_bytes=_vmem_limit(blocks + n_grp * n_cb * MOBA_BLOCK * tq * 4)),
        name="moba_attention",
    )(proj, proj, proj, proj)


F32_EXP2_ZERO_BELOW = -152.0


def _softplus2(z2):
    return jnp.maximum(z2, 0.0) + jnp.log2(1.0 + jnp.exp2(-jnp.abs(z2)))


def _suffix_sums(tri, x):
    hi = x.astype(BF16)
    lo = (x - hi.astype(F32)).astype(BF16)
    return (jnp.dot(tri, hi, preferred_element_type=F32)
            + jnp.dot(tri, lo, preferred_element_type=F32))


def _stick_kernel(q_ref, k_ref, v_ref, z_ref, tri_ref, o_ref, kb_scr, vt_scr, *, tq, n_grp):
    i = pl.program_id(2)
    scale2 = HEAD_DIM ** -0.5 * LOG2_E
    n_tiles = k_ref.shape[0] // tq
    heads = range(n_grp)

    @pl.when(i == 0)
    def _prepare_keys():
        def one_tile(j, carry):
            rows = pl.ds(pl.multiple_of(j * tq, tq), tq)
            for g in heads:
                kb_scr[g, rows, :] = k_ref[rows, _head_cols(g)].astype(BF16)
                vt_scr[g, j] = v_ref[rows, _head_cols(g)].T.astype(BF16)
            return carry
        lax.fori_loop(0, n_tiles, one_tile, 0)

    tri = tri_ref[...]
    qbs = [(q_ref[:, _head_cols(g)] * scale2).astype(BF16) for g in heads]
    causal = (lax.broadcasted_iota(jnp.int32, (tq, tq), 0)
              < lax.broadcasted_iota(jnp.int32, (tq, tq), 1))

    rows = pl.ds(pl.multiple_of(i * tq, tq), tq)
    zs = [lax.dot_general(kb_scr[g, rows, :], qbs[g], _TRANS_B, preferred_element_type=F32)
          for g in heads]
    sps = [_softplus2(z) for z in zs]
    drops = [jnp.where(causal, sp, 0.0) for sp in sps]
    removed = [jnp.sum(dr, axis=0, keepdims=True) for dr in drops]
    laters = [_suffix_sums(tri, dr) for dr in drops]
    a = [jnp.where(causal, jnp.exp2(zs[g] - sps[g] - laters[g]), 0.0).astype(BF16) for g in heads]
    accs = [jnp.dot(vt_scr[g, i], a[g], preferred_element_type=F32) for g in heads]

    def still_needed(removed):
        least = removed[0]
        for r in removed[1:]:
            least = jnp.minimum(least, r)
        return (-jnp.min(least) > F32_EXP2_ZERO_BELOW).astype(jnp.int32)

    def more_tiles(state):
        t, go = state[0], state[1]
        return jnp.logical_and(t < i, go > 0)

    def earlier_tile(state):
        t, _, removed, accs = state
        j = i - 1 - t
        rows = pl.ds(pl.multiple_of(j * tq, tq), tq)
        zs = [lax.dot_general(kb_scr[g, rows, :], qbs[g], _TRANS_B, preferred_element_type=F32)
              for g in heads]
        sps = [_softplus2(z) for z in zs]
        new_removed = [removed[g] + jnp.sum(sps[g], axis=0, keepdims=True) for g in heads]
        laters = [_suffix_sums(tri, sps[g]) + removed[g] for g in heads]
        a = [jnp.exp2(zs[g] - sps[g] - laters[g]).astype(BF16) for g in heads]
        new_accs = [accs[g] + jnp.dot(vt_scr[g, j], a[g], preferred_element_type=F32)
                    for g in heads]
        return (t + 1, still_needed(new_removed), tuple(new_removed), tuple(new_accs))

    state = lax.while_loop(more_tiles, earlier_tile,
                           (jnp.int32(0), jnp.int32(1), tuple(removed), tuple(accs)))
    for g in heads:
        o_ref[:, _head_cols(g)] = (state[3][g].T * _silu(z_ref[:, _head_cols(g)])).astype(o_ref.dtype)


def _stick(proj, b, s, width, col_q, col_k, col_v, col_z):
    n_grp = math.gcd(STICK_HEADS_PER_STEP, width // HEAD_DIM)
    n_steps = width // (HEAD_DIM * n_grp)
    tq = 256
    assert s % tq == 0
    nq = s // tq
    d = HEAD_DIM
    dg = d * n_grp
    assert col_q % n_grp == 0 and col_k % n_grp == 0 and col_v % n_grp == 0 and col_z % n_grp == 0
    tri = (jnp.arange(tq)[None, :] > jnp.arange(tq)[:, None]).astype(BF16)

    q_spec = lambda off: pl.BlockSpec((tq, dg), lambda bi, h, i: (bi * nq + i, off // n_grp + h))
    kv_spec = lambda off: pl.BlockSpec((s, dg), lambda bi, h, i: (bi, off // n_grp + h))
    blocks = 2 * s * dg * 4 + 3 * tq * dg * 4 + tq * tq * 2 + 2 * s * dg * 2
    return pl.pallas_call(
        functools.partial(_stick_kernel, tq=tq, n_grp=n_grp),
        grid=(b, n_steps, nq),
        in_specs=[q_spec(col_q), kv_spec(col_k), kv_spec(col_v), q_spec(col_z),
                  pl.BlockSpec((tq, tq), lambda bi, h, i: (0, 0))],
        out_specs=pl.BlockSpec((tq, dg), lambda bi, h, i: (bi * nq + i, h)),
        out_shape=jax.ShapeDtypeStruct((b * s, width), BF16),
        scratch_shapes=[pltpu.VMEM((n_grp, s, d), BF16),
                        pltpu.VMEM((n_grp, s // tq, d, tq), BF16)],
        compiler_params=pltpu.CompilerParams(
            dimension_semantics=("arbitrary", "arbitrary", "arbitrary"),
            vmem_limit_bytes=_vmem_limit(blocks)),
        name="stick_breaking_attention",
    )(proj, proj, proj, proj, tri)


def _conv_kernel(bb_ref, bc_ref, bx_ref, bz_ref, hc_ref, hx_ref, w_ref, o_ref):
    i = pl.program_id(1)
    xc = bc_ref[...] * bx_ref[...]
    halo = jnp.where(i == 0, 0.0, hc_ref[...] * hx_ref[...])
    full = jnp.concatenate([halo, xc], axis=0)
    prev1 = pltpu.roll(full, 1, 0)[SUBLANES:]
    prev2 = pltpu.roll(full, 2, 0)[SUBLANES:]
    w = w_ref[...]
    y = w[0:1, :] * prev2 + w[1:2, :] * prev1 + w[2:3, :] * xc
    o_ref[...] = (bb_ref[...] * y * _silu(bz_ref[...])).astype(o_ref.dtype)


def _conv_branch(proj, conv_w, b, s, width, col_bb, col_bc, col_bx, col_bz):
    ts = _tile(s, 512)
    tw = _tile(width, 512)
    ns = s // ts
    nw = width // tw
    halo_blocks = ts // SUBLANES

    main = lambda off: pl.BlockSpec((ts, tw), lambda bi, i, j: (bi * ns + i, off * nw + j))
    halo = lambda off: pl.BlockSpec(
        (SUBLANES, tw),
        lambda bi, i, j: (jnp.maximum((bi * ns + i) * halo_blocks - 1, 0), off * nw + j))
    return pl.pallas_call(
        _conv_kernel,
        grid=(b, ns, nw),
        in_specs=[main(col_bb), main(col_bc), main(col_bx), main(col_bz),
                  halo(col_bc), halo(col_bx),
                  pl.BlockSpec((CONV_WIDTH, tw), lambda bi, i, j: (0, j))],
        out_specs=pl.BlockSpec((ts, tw), lambda bi, i, j: (bi * ns + i, j)),
        out_shape=jax.ShapeDtypeStruct((b * s, width), BF16),
        compiler_params=pltpu.CompilerParams(
            dimension_semantics=("arbitrary", "arbitrary", "arbitrary")),
        name="short_conv_branch",
    )(proj, proj, proj, proj, proj, proj, conv_w)


def _merge_kernel(ua_ref, ub_ref, uc_ref, pa_ref, pb_ref, pc_ref,
                  ga_ref, gb_ref, gc_ref, ba_ref, bb_ref, bc_ref, o_ref,
                  pa_scr, pb_scr, pc_scr):
    @pl.when(pl.program_id(1) == 0)
    def _cast_weights():
        pa_scr[...] = pa_ref[...].astype(BF16)
        pb_scr[...] = pb_ref[...].astype(BF16)
        pc_scr[...] = pc_ref[...].astype(BF16)

    def branch(u_ref, p_scr, g_ref, b_ref):
        gate = _sigmoid(g_ref[...] + b_ref[...])
        return gate * jnp.dot(u_ref[...], p_scr[...], preferred_element_type=F32)
    y = (branch(ua_ref, pa_scr, ga_ref, ba_ref) + branch(ub_ref, pb_scr, gb_ref, bb_ref)
         + branch(uc_ref, pc_scr, gc_ref, bc_ref))
    o_ref[...] = y.astype(o_ref.dtype)


def _merge(ua, ub, uc, pa, pb, pc, layer, proj, b_merge, d_model, col_g):
    m, width = ua.shape
    bm = _tile(m, 512)
    bn = _tile(d_model, 512)
    nn = d_model // bn

    u_spec = pl.BlockSpec((bm, width), lambda j, i: (i, 0))
    p_spec = pl.BlockSpec((None, width, bn), lambda j, i: (layer, 0, j))
    g_spec = lambda br: pl.BlockSpec((bm, bn), lambda j, i: (i, col_g + br * nn + j))
    b_spec = lambda br: pl.BlockSpec((1, bn), lambda j, i: (0, br * nn + j))
    blocks = 3 * bm * width * 2 + 3 * width * bn * 4 + 3 * bm * bn * 4 + bm * bn * 2
    return pl.pallas_call(
        _merge_kernel,
        grid=(nn, m // bm),
        in_specs=[u_spec, u_spec, u_spec, p_spec, p_spec, p_spec,
                  g_spec(0), g_spec(1), g_spec(2), b_spec(0), b_spec(1), b_spec(2)],
        out_specs=pl.BlockSpec((bm, bn), lambda j, i: (i, j)),
        out_shape=jax.ShapeDtypeStruct((m, d_model), BF16),
        scratch_shapes=[pltpu.VMEM((width, bn), BF16)] * N_BRANCHES,
        compiler_params=pltpu.CompilerParams(
            dimension_semantics=("arbitrary", "arbitrary"),
            vmem_limit_bytes=_vmem_limit(blocks + 3 * width * bn)),
        name="gated_merge",
    )(ua, ub, uc, pa, pb, pc, proj, proj, proj, b_merge, b_merge, b_merge)


def _out_kernel(y_ref, w_ref, x_ref, g_ref, o_ref, *, bn):
    j = pl.program_id(1)
    cols = pl.ds(pl.multiple_of(j * bn, bn), bn)
    o_ref[:, cols] = jnp.dot(y_ref[...], w_ref[...], preferred_element_type=F32)

    @pl.when(j == pl.num_programs(1) - 1)
    def _normalize():
        out = o_ref[...]
        inv = lax.rsqrt(jnp.mean(out * out, axis=-1, keepdims=True) + RMS_EPS)
        o_ref[...] = x_ref[...] + out * inv * g_ref[...]


def _out_proj(y, w_o, x2d, gain):
    m, d = y.shape
    bm = _tile(m, 512)
    bn = _tile(d, 512)
    blocks = bm * d * 2 + d * bn * 2 + 2 * bm * d * 4
    return pl.pallas_call(
        functools.partial(_out_kernel, bn=bn),
        grid=(m // bm, d // bn),
        in_specs=[pl.BlockSpec((bm, d), lambda i, j: (i, 0)),
                  pl.BlockSpec((d, bn), lambda i, j: (0, j)),
                  pl.BlockSpec((bm, d), lambda i, j: (i, 0)),
                  pl.BlockSpec((1, d), lambda i, j: (0, 0))],
        out_specs=pl.BlockSpec((bm, d), lambda i, j: (i, 0)),
        out_shape=jax.ShapeDtypeStruct((m, d), F32),
        compiler_params=pltpu.CompilerParams(
            dimension_semantics=("arbitrary", "arbitrary"),
            vmem_limit_bytes=_vmem_limit(blocks)),
        name="out_proj_norm_residual",
    )(y, w_o, x2d, gain.reshape(1, d))


def kernel(x, pre_norm_gain, post_norm_gain, w_in, b_merge_gate, conv_w,
           w_branch_a, w_branch_b, w_branch_c, w_out):
    b, s, d_model = x.shape
    depth = w_in.shape[0]
    width = conv_w.shape[-1]
    hw = width // HEAD_DIM
    assert s % MOBA_BLOCK == 0 and d_model % 128 == 0
    tables = _rope_tables(s)
    x2d = x.reshape(b * s, d_model)
    w_in_bf16 = w_in[0].astype(BF16)
    bn = _tile(d_model, 512)

    for layer in range(depth):
        h = _rmsnorm(x2d, pre_norm_gain[layer])
        casts = [(w_out, layer)]
        if layer + 1 < depth:
            casts.append((w_in, layer + 1))
        proj, cast_outs = _in_proj(h, w_in_bf16, tables, 2 * width, casts)
        w_o = cast_outs[0]
        if layer + 1 < depth:
            w_in_bf16 = cast_outs[1]

        ua = _moba(proj, b, s, width, 0 * hw, 1 * hw, 2 * hw, 3 * hw)
        ub = _conv_branch(proj, conv_w[layer], b, s, width, 4, 5, 6, 7)
        uc = _stick(proj, b, s, width, 8 * hw, 9 * hw, 10 * hw, 11 * hw)
        y = _merge(ua, ub, uc, w_branch_a, w_branch_b, w_branch_c, layer, proj,
                   b_merge_gate[layer].reshape(1, -1), d_model, (12 * width) // bn)
        x2d = _out_proj(y, w_o, x2d, post_norm_gain[layer])
    return x2d.reshape(b, s, d_model)
```

```python
import functools
import math

import jax
import jax.numpy as jnp
from jax import lax
from jax.experimental import pallas as pl
from jax.experimental.pallas import tpu as pltpu

HEAD_DIM = 128
MOBA_BLOCK = 256
MOBA_TOPK = 3
ROPE_THETA = 500000.0
ROPE_DIM = HEAD_DIM // 4
CONV_WIDTH = 3
N_BRANCHES = 3
RMS_EPS = 1e-6

VMEM_REQUEST_CAP = 60000 * 1024
SUBLANES = 8

RMSNORM_ROW_TILE = 256
IN_PROJ_TILE = 1024
STICK_TILE = 256
CONV_TILE = 512
MERGE_TILE = 512
OUT_PROJ_TILE = 512
NORM_SLAB_ROWS = 64

F32 = jnp.float32
BF16 = jnp.bfloat16


def _vmem_limit(block_bytes):
    return int(min(VMEM_REQUEST_CAP, max(32 * 1024 * 1024, 2 * block_bytes + 8 * 1024 * 1024)))


def _tile(n, target):
    if n <= target:
        return n
    t = target - target % 128
    while t >= 128:
        if n % t == 0:
            return t
        t -= 128
    return n


def _sigmoid(x):
    return 1.0 / (1.0 + jnp.exp(-x))


def _silu(x):
    return x * _sigmoid(x)


def _rmsnorm_kernel(x_ref, g_ref, o_ref):
    x = x_ref[...]
    inv = lax.rsqrt(jnp.mean(x * x, axis=-1, keepdims=True) + RMS_EPS)
    o_ref[...] = (x * inv * g_ref[...]).astype(o_ref.dtype)


def _rmsnorm(x2d, gain):
    m, d = x2d.shape
    bm = _tile(m, RMSNORM_ROW_TILE)
    return pl.pallas_call(
        _rmsnorm_kernel,
        grid=(m // bm,),
        in_specs=[pl.BlockSpec((bm, d), lambda i: (i, 0)),
                  pl.BlockSpec((1, d), lambda i: (0, 0))],
        out_specs=pl.BlockSpec((bm, d), lambda i: (i, 0)),
        out_shape=jax.ShapeDtypeStruct((m, d), BF16),
        compiler_params=pltpu.CompilerParams(dimension_semantics=("arbitrary",)),
        name="rmsnorm_pre",
    )(x2d, gain.reshape(1, d))


def _rope_tables(s):
    half = ROPE_DIM // 2
    inv_freq = ROPE_THETA ** (-jnp.arange(half, dtype=F32) / half)
    ang = jnp.arange(s).astype(F32)[:, None] * inv_freq[None, :]
    cos = jnp.cos(ang)
    sin = jnp.sin(ang)
    rest = HEAD_DIM - ROPE_DIM
    zeros_h = jnp.zeros((s, half), F32)
    cos_t = jnp.concatenate([cos, cos, jnp.ones((s, rest), F32)], axis=-1)
    sin_up = jnp.concatenate([zeros_h, sin, jnp.zeros((s, rest), F32)], axis=-1)
    sin_dn = jnp.concatenate([-sin, zeros_h, jnp.zeros((s, rest), F32)], axis=-1)
    return cos_t, sin_up, sin_dn


def _rope(x, cos_t, sin_up, sin_dn):
    half = ROPE_DIM // 2
    return (x * cos_t + pltpu.roll(x, half, 1) * sin_up
            + pltpu.roll(x, HEAD_DIM - half, 1) * sin_dn)


def _in_proj_kernel(x_ref, w_ref, *refs, n_casts, has_first, n_col_tiles):
    cast_srcs = refs[:n_casts]
    o_ref = refs[n_casts + has_first]
    cast_dsts = refs[n_casts + has_first + 1:n_casts + has_first + 1 + n_casts]
    for src, dst in zip(cast_srcs, cast_dsts):
        dst[...] = src[...].astype(BF16)

    i, j = pl.program_id(0), pl.program_id(1)
    in_range = j < n_col_tiles
    if has_first:
        first_hbm, sem = refs[n_casts], refs[-1]
        bn = o_ref.shape[1]

        @pl.when(jnp.logical_and(in_range, i == 0))
        def _copy_first_rows():
            cols = pl.ds(pl.multiple_of(j * bn, bn), bn)
            copy = pltpu.make_async_copy(first_hbm.at[:, cols], o_ref, sem)
            copy.start()
            copy.wait()
        in_range = jnp.logical_and(in_range, i > 0)

    @pl.when(in_range)
    def _project():
        o_ref[...] = jnp.dot(x_ref[...], w_ref[...], preferred_element_type=F32)


BF16_ROW_TILE = 16
LANES = 128


def _cast_blocks(r, wdt, max_blocks):
    options = [(r, c) for c in range(LANES, wdt + 1, LANES) if wdt % c == 0 and wdt // c <= max_blocks]
    options += [(q, wdt) for q in range(BF16_ROW_TILE, r + 1, BF16_ROW_TILE)
                if r % q == 0 and r // q <= max_blocks]
    return min(options, key=lambda rc: rc[0] * rc[1])


def _in_proj(h, w_bf16, casts, first_rows=None):
    m, k = h.shape
    _, n = w_bf16.shape
    bm = _tile(m, IN_PROJ_TILE)
    bn = _tile(n, IN_PROJ_TILE)
    has_first = first_rows is not None
    n_rows = m // bm
    nj = n // bn
    max_cols = nj + nj // 4
    plans = [_cast_blocks(st.shape[1], st.shape[2], n_rows * max_cols) for st, _ in casts]
    n_blocks = [(st.shape[1] // rb) * (st.shape[2] // cb) for (st, _), (rb, cb) in zip(casts, plans)]
    grid_cols = max([nj] + [-(-nb // n_rows) for nb in n_blocks])

    col = lambda j: jnp.minimum(j, nj - 1)
    w_col = (lambda i, j: jnp.where(i == 0, 0, col(j))) if has_first else (lambda i, j: col(j))
    in_specs = [pl.BlockSpec((bm, k), lambda i, j: (i, 0)),
                pl.BlockSpec((k, bn), lambda i, j: (0, w_col(i, j)))]
    out_specs = [pl.BlockSpec((bm, bn), lambda i, j: (i, col(j)))]
    out_shape = [jax.ShapeDtypeStruct((m, n), F32)]
    args = [h, w_bf16]
    for (stack, layer), (rb, cb), nb in zip(casts, plans, n_blocks):
        _, r, wdt = stack.shape
        by_cols = rb == r

        def block_index(i, j, nb=nb, by_cols=by_cols):
            t = jnp.minimum(i * grid_cols + j, nb - 1)
            return (0, t) if by_cols else (t, 0)

        in_specs.append(pl.BlockSpec(
            (None, rb, cb), lambda i, j, layer=layer, f=block_index: (layer, *f(i, j))))
        out_specs.append(pl.BlockSpec((rb, cb), block_index))
        out_shape.append(jax.ShapeDtypeStruct((r, wdt), BF16))
        args.append(stack)
    if has_first:
        in_specs.append(pl.BlockSpec(memory_space=pl.ANY))
        args.append(first_rows)
    outs = pl.pallas_call(
        functools.partial(_in_proj_kernel, n_casts=len(casts), has_first=int(has_first),
                          n_col_tiles=nj),
        grid=(n_rows, grid_cols),
        in_specs=in_specs, out_specs=out_specs, out_shape=out_shape,
        scratch_shapes=[pltpu.SemaphoreType.DMA(())] if has_first else [],
        compiler_params=pltpu.CompilerParams(
            dimension_semantics=("arbitrary", "arbitrary"),
            vmem_limit_bytes=VMEM_REQUEST_CAP),
        name="in_proj",
    )(*args)
    return outs[0], outs[1:]


def _in_proj_first_rows_kernel(x_ref, w_ref, o_ref, wb_ref):
    wb = w_ref[...].astype(BF16)
    wb_ref[...] = wb
    o_ref[...] = jnp.dot(x_ref[...], wb, preferred_element_type=F32)


def _in_proj_first_rows(h, w_stack, layer):
    m, k = h.shape
    _, _, n = w_stack.shape
    bm = _tile(m, IN_PROJ_TILE)
    bn = _tile(n, IN_PROJ_TILE // 2)
    blocks = bm * k * 2 + k * bn * 4 + k * bn * 2 + bm * bn * 4
    return pl.pallas_call(
        _in_proj_first_rows_kernel,
        grid=(n // bn,),
        in_specs=[pl.BlockSpec((bm, k), lambda j: (0, 0)),
                  pl.BlockSpec((None, k, bn), lambda j: (layer, 0, j))],
        out_specs=[pl.BlockSpec((bm, bn), lambda j: (0, j)),
                   pl.BlockSpec((k, bn), lambda j: (0, j))],
        out_shape=[jax.ShapeDtypeStruct((bm, n), F32), jax.ShapeDtypeStruct((k, n), BF16)],
        compiler_params=pltpu.CompilerParams(
            dimension_semantics=("arbitrary",),
            vmem_limit_bytes=_vmem_limit(blocks)),
        name="in_proj_first_rows",
    )(h, w_stack)


_TRANS_B = (((1,), (1,)), ((), ()))


MOBA_HEADS_PER_STEP = 2
STICK_HEADS_PER_STEP = 4
MOBA_CHUNK_BLOCKS = 4
LOG2_E = 1.4426950408889634


def _head_cols(g):
    return slice(g * HEAD_DIM, (g + 1) * HEAD_DIM)


def _moba_kernel(q_ref, k_ref, v_ref, z_ref, cq_ref, uq_ref, dq_ref, ck_ref, uk_ref, dk_ref,
                 o_ref, kr_scr, vt_blk_scr, vt_chunk_scr, kmean_scr, bias_scr, s0_scr, s1_scr,
                 *, n_blk, tq, n_grp, n_cb):
    i = pl.program_id(2)
    blk = MOBA_BLOCK
    chunk = n_cb * blk
    scale2 = HEAD_DIM ** -0.5 * LOG2_E
    neg_inf = -jnp.inf
    heads = range(n_grp)

    @pl.when(i == 0)
    def _prepare_keys():
        def one_chunk(c, carry):
            for u in range(n_cb):
                j = c * n_cb + u
                rows = pl.ds(pl.multiple_of(j * blk, blk), blk)
                ck, uk, dk = ck_ref[rows, :], uk_ref[rows, :], dk_ref[rows, :]
                for g in heads:
                    kr = _rope(k_ref[rows, _head_cols(g)], ck, uk, dk)
                    kr_scr[g, rows, :] = kr.astype(BF16)
                    vt = v_ref[rows, _head_cols(g)].T.astype(BF16)
                    vt_blk_scr[g, j] = vt
                    vt_chunk_scr[g, c, :, u * blk:(u + 1) * blk] = vt
                    kmean_scr[g, pl.ds(j, 1), :] = jnp.mean(kr, axis=0, keepdims=True)
            return carry
        lax.fori_loop(0, n_blk // n_cb, one_chunk, 0)

    own = i
    own_rows = pl.ds(pl.multiple_of(own * blk, blk), blk)
    blk_id = lax.broadcasted_iota(jnp.int32, (n_blk, tq), 0).astype(F32)
    past = blk_id < own.astype(F32)
    causal = (lax.broadcasted_iota(jnp.int32, (blk, tq), 0)
              <= lax.broadcasted_iota(jnp.int32, (blk, tq), 1))

    cq, uq, dq = cq_ref[...], uq_ref[...], dq_ref[...]
    qs = [_rope(q_ref[:, _head_cols(g)], cq, uq, dq) for g in heads]
    qbs = [(q * scale2).astype(BF16) for q in qs]
    ss = [lax.dot_general(kr_scr[g, own_rows, :], qbs[g], _TRANS_B, preferred_element_type=F32)
          for g in heads]
    gates = [lax.dot_general(kmean_scr[g], qs[g], _TRANS_B, precision=lax.Precision.HIGHEST,
                             preferred_element_type=F32) for g in heads]
    gscs = [jnp.where(past, gate, neg_inf) for gate in gates]
    sels = [jnp.zeros((n_blk, tq), jnp.bool_) for _ in heads]
    for _ in range(min(MOBA_TOPK, n_blk)):
        gmaxs = [jnp.max(gsc, axis=0, keepdims=True) for gsc in gscs]
        firsts = [jnp.min(jnp.where(gscs[g] == gmaxs[g], blk_id, float(n_blk)), axis=0,
                          keepdims=True) for g in heads]
        picks = [blk_id == first for first in firsts]
        sels = [jnp.logical_or(sels[g], picks[g]) for g in heads]
        gscs = [jnp.where(picks[g], neg_inf, gscs[g]) for g in heads]
    for g in heads:
        bias_scr[g] = jnp.where(jnp.logical_and(sels[g], past), 0.0, neg_inf)

    ss = [jnp.where(causal, s, neg_inf) for s in ss]
    ms = [jnp.max(s, axis=0, keepdims=True) for s in ss]
    ps = [jnp.exp2(ss[g] - ms[g]) for g in heads]
    ls = [jnp.sum(p, axis=0, keepdims=True) for p in ps]
    accs = [jnp.dot(vt_blk_scr[g, own], ps[g].astype(BF16), preferred_element_type=F32)
            for g in heads]

    n_chunks = (own + n_cb - 1) // n_cb
    last_chunk = n_blk // n_cb - 1

    def score_chunk(c, dst_scr):
        c = jnp.minimum(c, last_chunk)
        rows = pl.ds(pl.multiple_of(c * chunk, chunk), chunk)
        for g in heads:
            sc = lax.dot_general(kr_scr[g, rows, :], qbs[g], _TRANS_B,
                                 preferred_element_type=F32)
            for u in range(n_cb):
                dst_scr[g, u * blk:(u + 1) * blk, :] = (
                    sc[u * blk:(u + 1) * blk] + bias_scr[g, pl.ds(c * n_cb + u, 1), :])

    def absorb_chunk(c, src_scr, state):
        ms, ls, accs = state
        m_news, new_ls, new_accs = [], [], []
        for g in heads:
            s = src_scr[g]
            m_new = jnp.maximum(ms[g], jnp.max(s, axis=0, keepdims=True))
            alpha = jnp.exp2(ms[g] - m_new)
            p = jnp.exp2(s - m_new)
            new_ls.append(alpha * ls[g] + jnp.sum(p, axis=0, keepdims=True))
            new_accs.append(alpha * accs[g] + jnp.dot(vt_chunk_scr[g, c], p.astype(BF16),
                                                      preferred_element_type=F32))
            m_news.append(m_new)
        return tuple(m_news), tuple(new_ls), tuple(new_accs)

    score_chunk(0, s0_scr)

    def chunk_pair(p, state):
        c = 2 * p
        score_chunk(c + 1, s1_scr)
        state = absorb_chunk(c, s0_scr, state)

        def second(state):
            score_chunk(c + 2, s0_scr)
            return absorb_chunk(c + 1, s1_scr, state)

        return lax.cond(c + 1 < n_chunks, second, lambda st: st, state)

    _, ls, accs = lax.fori_loop(0, (n_chunks + 1) // 2, chunk_pair,
                                (tuple(ms), tuple(ls), tuple(accs)))
    for g in heads:
        out = (accs[g] * (1.0 / ls[g])).T
        o_ref[:, _head_cols(g)] = (out * _silu(z_ref[:, _head_cols(g)])).astype(o_ref.dtype)


def _moba(proj, tables, b, s, width, col_q, col_k, col_v, col_z):
    n_grp = MOBA_HEADS_PER_STEP
    n_steps = width // (HEAD_DIM * n_grp)
    n_blk = s // MOBA_BLOCK
    n_cb = MOBA_CHUNK_BLOCKS
    assert n_blk % n_cb == 0
    tq = MOBA_BLOCK
    nq = s // tq
    d = HEAD_DIM
    dg = d * n_grp
    assert col_q % n_grp == 0 and col_k % n_grp == 0 and col_v % n_grp == 0 and col_z % n_grp == 0

    q_spec = lambda off: pl.BlockSpec((tq, dg), lambda bi, h, i: (bi * nq + i, off // n_grp + h))
    kv_spec = lambda off: pl.BlockSpec((s, dg), lambda bi, h, i: (bi, off // n_grp + h))
    tq_spec = pl.BlockSpec((tq, d), lambda bi, h, i: (i, 0))
    tk_spec = pl.BlockSpec((s, d), lambda bi, h, i: (0, 0))
    blocks = (2 * s * dg * 4 + 3 * s * d * 4 + 2 * tq * dg * 4 + 3 * tq * d * 4 + tq * dg * 2
              + 3 * s * dg * 2)
    return pl.pallas_call(
        functools.partial(_moba_kernel, n_blk=n_blk, tq=tq, n_grp=n_grp, n_cb=n_cb),
        grid=(b, n_steps, nq),
        in_specs=[q_spec(col_q), kv_spec(col_k), kv_spec(col_v), q_spec(col_z),
                  tq_spec, tq_spec, tq_spec, tk_spec, tk_spec, tk_spec],
        out_specs=pl.BlockSpec((tq, dg), lambda bi, h, i: (bi * nq + i, h)),
        out_shape=jax.ShapeDtypeStruct((b * s, width), BF16),
        scratch_shapes=[pltpu.VMEM((n_grp, s, d), BF16),
                        pltpu.VMEM((n_grp, n_blk, d, MOBA_BLOCK), BF16),
                        pltpu.VMEM((n_grp, n_blk // n_cb, d, n_cb * MOBA_BLOCK), BF16),
                        pltpu.VMEM((n_grp, n_blk, d), F32),
                        pltpu.VMEM((n_grp, n_blk, tq), F32),
                        pltpu.VMEM((n_grp, n_cb * MOBA_BLOCK, tq), F32),
                        pltpu.VMEM((n_grp, n_cb * MOBA_BLOCK, tq), F32)],
        compiler_params=pltpu.CompilerParams(
            dimension_semantics=("arbitrary", "arbitrary", "arbitrary"),
            vmem_limit_bytes=_vmem_limit(blocks + n_grp * n_cb * MOBA_BLOCK * tq * 4)),
        name="moba_attention",
    )(proj, proj, proj, proj, *tables, *tables)


F32_EXP2_ZERO_BELOW = -152.0


def _softplus2(z2):
    return jnp.maximum(z2, 0.0) + jnp.log2(1.0 + jnp.exp2(-jnp.abs(z2)))


def _suffix_sums(tri, x):
    return jnp.dot(tri, x.astype(BF16), preferred_element_type=F32)


def _stick_kernel(q_ref, k_ref, v_ref, z_ref, tri_ref, o_ref, kb_scr, vt_scr, *, tq, n_grp):
    i = pl.program_id(2)
    scale2 = HEAD_DIM ** -0.5 * LOG2_E
    n_tiles = k_ref.shape[0] // tq
    heads = range(n_grp)

    @pl.when(i == 0)
    def _prepare_keys():
        def one_tile(j, carry):
            rows = pl.ds(pl.multiple_of(j * tq, tq), tq)
            for g in heads:
                kb_scr[g, rows, :] = k_ref[rows, _head_cols(g)].astype(BF16)
                vt_scr[g, j] = v_ref[rows, _head_cols(g)].T.astype(BF16)
            return carry
        lax.fori_loop(0, n_tiles, one_tile, 0)

    tri = tri_ref[...]
    qbs = [(q_ref[:, _head_cols(g)] * scale2).astype(BF16) for g in heads]
    causal = (lax.broadcasted_iota(jnp.int32, (tq, tq), 0)
              < lax.broadcasted_iota(jnp.int32, (tq, tq), 1))

    rows = pl.ds(pl.multiple_of(i * tq, tq), tq)
    zs = [lax.dot_general(kb_scr[g, rows, :], qbs[g], _TRANS_B, preferred_element_type=F32)
          for g in heads]
    sps = [_softplus2(z) for z in zs]
    drops = [jnp.where(causal, sp, 0.0) for sp in sps]
    laters = [_suffix_sums(tri, dr) for dr in drops]
    removed = [laters[g][0:1, :] + drops[g][0:1, :] for g in heads]
    a = [jnp.where(causal, jnp.exp2(zs[g] - sps[g] - laters[g]), 0.0).astype(BF16) for g in heads]
    accs = [jnp.dot(vt_scr[g, i], a[g], preferred_element_type=F32) for g in heads]

    def still_needed(removed):
        least = removed[0]
        for r in removed[1:]:
            least = jnp.minimum(least, r)
        return (-jnp.min(least) > F32_EXP2_ZERO_BELOW).astype(jnp.int32)

    def more_tiles(state):
        t, go = state[0], state[1]
        return jnp.logical_and(t < i, go > 0)

    def earlier_tile(state):
        t, _, removed, accs = state
        j = i - 1 - t
        rows = pl.ds(pl.multiple_of(j * tq, tq), tq)
        zs = [lax.dot_general(kb_scr[g, rows, :], qbs[g], _TRANS_B, preferred_element_type=F32)
              for g in heads]
        sps = [_softplus2(z) for z in zs]
        within = [_suffix_sums(tri, sp) for sp in sps]
        new_removed = [removed[g] + within[g][0:1, :] + sps[g][0:1, :] for g in heads]
        a = [jnp.exp2(zs[g] - sps[g] - within[g] - removed[g]).astype(BF16) for g in heads]
        new_accs = [accs[g] + jnp.dot(vt_scr[g, j], a[g], preferred_element_type=F32)
                    for g in heads]
        return (t + 1, still_needed(new_removed), tuple(new_removed), tuple(new_accs))

    state = lax.while_loop(more_tiles, earlier_tile,
                           (jnp.int32(0), jnp.int32(1), tuple(removed), tuple(accs)))
    for g in heads:
        o_ref[:, _head_cols(g)] = (state[3][g].T * _silu(z_ref[:, _head_cols(g)])).astype(o_ref.dtype)


def _stick(proj, b, s, width, col_q, col_k, col_v, col_z):
    n_grp = math.gcd(STICK_HEADS_PER_STEP, width // HEAD_DIM)
    n_steps = width // (HEAD_DIM * n_grp)
    tq = STICK_TILE
    assert s % tq == 0
    nq = s // tq
    d = HEAD_DIM
    dg = d * n_grp
    assert col_q % n_grp == 0 and col_k % n_grp == 0 and col_v % n_grp == 0 and col_z % n_grp == 0
    tri = (jnp.arange(tq)[None, :] > jnp.arange(tq)[:, None]).astype(BF16)

    q_spec = lambda off: pl.BlockSpec((tq, dg), lambda bi, h, i: (bi * nq + i, off // n_grp + h))
    kv_spec = lambda off: pl.BlockSpec((s, dg), lambda bi, h, i: (bi, off // n_grp + h))
    blocks = 2 * s * dg * 4 + 3 * tq * dg * 4 + tq * tq * 2 + 2 * s * dg * 2
    return pl.pallas_call(
        functools.partial(_stick_kernel, tq=tq, n_grp=n_grp),
        grid=(b, n_steps, nq),
        in_specs=[q_spec(col_q), kv_spec(col_k), kv_spec(col_v), q_spec(col_z),
                  pl.BlockSpec((tq, tq), lambda bi, h, i: (0, 0))],
        out_specs=pl.BlockSpec((tq, dg), lambda bi, h, i: (bi * nq + i, h)),
        out_shape=jax.ShapeDtypeStruct((b * s, width), BF16),
        scratch_shapes=[pltpu.VMEM((n_grp, s, d), BF16),
                        pltpu.VMEM((n_grp, s // tq, d, tq), BF16)],
        compiler_params=pltpu.CompilerParams(
            dimension_semantics=("arbitrary", "arbitrary", "arbitrary"),
            vmem_limit_bytes=_vmem_limit(blocks)),
        name="stick_breaking_attention",
    )(proj, proj, proj, proj, tri)


def _conv_kernel(bb_ref, bc_ref, bx_ref, bz_ref, hc_ref, hx_ref, w_ref, o_ref):
    i = pl.program_id(1)
    xc = bc_ref[...] * bx_ref[...]
    halo = jnp.where(i == 0, 0.0, hc_ref[...] * hx_ref[...])
    full = jnp.concatenate([halo, xc], axis=0)
    prev1 = pltpu.roll(full, 1, 0)[SUBLANES:]
    prev2 = pltpu.roll(full, 2, 0)[SUBLANES:]
    w = w_ref[...]
    y = w[0:1, :] * prev2 + w[1:2, :] * prev1 + w[2:3, :] * xc
    o_ref[...] = (bb_ref[...] * y * _silu(bz_ref[...])).astype(o_ref.dtype)


def _conv_branch(proj, conv_w, b, s, width, col_bb, col_bc, col_bx, col_bz):
    ts = _tile(s, CONV_TILE)
    tw = _tile(width, CONV_TILE)
    ns = s // ts
    nw = width // tw
    halo_blocks = ts // SUBLANES

    main = lambda off: pl.BlockSpec((ts, tw), lambda bi, i, j: (bi * ns + i, off * nw + j))
    halo = lambda off: pl.BlockSpec(
        (SUBLANES, tw),
        lambda bi, i, j: (jnp.maximum((bi * ns + i) * halo_blocks - 1, 0), off * nw + j))
    return pl.pallas_call(
        _conv_kernel,
        grid=(b, ns, nw),
        in_specs=[main(col_bb), main(col_bc), main(col_bx), main(col_bz),
                  halo(col_bc), halo(col_bx),
                  pl.BlockSpec((CONV_WIDTH, tw), lambda bi, i, j: (0, j))],
        out_specs=pl.BlockSpec((ts, tw), lambda bi, i, j: (bi * ns + i, j)),
        out_shape=jax.ShapeDtypeStruct((b * s, width), BF16),
        compiler_params=pltpu.CompilerParams(
            dimension_semantics=("arbitrary", "arbitrary", "arbitrary")),
        name="short_conv_branch",
    )(proj, proj, proj, proj, proj, proj, conv_w)


def _merge_kernel(ua_ref, ub_ref, uc_ref, pa_ref, pb_ref, pc_ref,
                  ga_ref, gb_ref, gc_ref, ba_ref, bb_ref, bc_ref, o_ref,
                  pa_scr, pb_scr, pc_scr):
    @pl.when(pl.program_id(1) == 0)
    def _cast_weights():
        pa_scr[...] = pa_ref[...].astype(BF16)
        pb_scr[...] = pb_ref[...].astype(BF16)
        pc_scr[...] = pc_ref[...].astype(BF16)

    def branch(u_ref, p_scr, g_ref, b_ref):
        gate = _sigmoid(g_ref[...] + b_ref[...])
        return gate * jnp.dot(u_ref[...], p_scr[...], preferred_element_type=F32)
    y = (branch(ua_ref, pa_scr, ga_ref, ba_ref) + branch(ub_ref, pb_scr, gb_ref, bb_ref)
         + branch(uc_ref, pc_scr, gc_ref, bc_ref))
    o_ref[...] = y.astype(o_ref.dtype)


def _merge(ua, ub, uc, pa, pb, pc, layer, proj, b_merge, d_model, gate_col):
    m, width = ua.shape
    bm = _tile(m, MERGE_TILE)
    bn = _tile(d_model, MERGE_TILE)
    nn = d_model // bn
    assert gate_col % bn == 0
    col_g = gate_col // bn

    u_spec = pl.BlockSpec((bm, width), lambda j, i: (i, 0))
    p_spec = pl.BlockSpec((None, width, bn), lambda j, i: (layer, 0, j))
    g_spec = lambda br: pl.BlockSpec((bm, bn), lambda j, i: (i, col_g + br * nn + j))
    b_spec = lambda br: pl.BlockSpec((1, bn), lambda j, i: (0, br * nn + j))
    blocks = 3 * bm * width * 2 + 3 * width * bn * 4 + 3 * bm * bn * 4 + bm * bn * 2
    return pl.pallas_call(
        _merge_kernel,
        grid=(nn, m // bm),
        in_specs=[u_spec, u_spec, u_spec, p_spec, p_spec, p_spec,
                  g_spec(0), g_spec(1), g_spec(2), b_spec(0), b_spec(1), b_spec(2)],
        out_specs=pl.BlockSpec((bm, bn), lambda j, i: (i, j)),
        out_shape=jax.ShapeDtypeStruct((m, d_model), BF16),
        scratch_shapes=[pltpu.VMEM((width, bn), BF16)] * N_BRANCHES,
        compiler_params=pltpu.CompilerParams(
            dimension_semantics=("arbitrary", "arbitrary"),
            vmem_limit_bytes=_vmem_limit(blocks + 3 * width * bn)),
        name="gated_merge",
    )(ua, ub, uc, pa, pb, pc, proj, proj, proj, b_merge, b_merge, b_merge)


def _out_kernel(y_ref, w_ref, x_ref, g_ref, *refs, bn, with_next):
    j = pl.program_id(1)
    o_ref = refs[-2] if with_next else refs[-1]
    cols = pl.ds(pl.multiple_of(j * bn, bn), bn)
    o_ref[:, cols] = jnp.dot(y_ref[...], w_ref[...], preferred_element_type=F32)

    @pl.when(j == pl.num_programs(1) - 1)
    def _normalize():
        def one_slab(r, carry):
            rows = pl.ds(pl.multiple_of(r * NORM_SLAB_ROWS, NORM_SLAB_ROWS), NORM_SLAB_ROWS)
            out = o_ref[rows, :]
            inv = lax.rsqrt(jnp.mean(out * out, axis=-1, keepdims=True) + RMS_EPS)
            x_new = x_ref[rows, :] + out * inv * g_ref[...]
            o_ref[rows, :] = x_new
            if with_next:
                gn_ref, h_ref = refs[0], refs[-1]
                inv_n = lax.rsqrt(jnp.mean(x_new * x_new, axis=-1, keepdims=True) + RMS_EPS)
                h_ref[rows, :] = (x_new * inv_n * gn_ref[...]).astype(h_ref.dtype)
            return carry
        lax.fori_loop(0, o_ref.shape[0] // NORM_SLAB_ROWS, one_slab, 0)


def _out_proj(y, w_o, x2d, gain, next_gain=None):
    m, d = y.shape
    bm = _tile(m, OUT_PROJ_TILE)
    bn = _tile(d, OUT_PROJ_TILE)
    with_next = next_gain is not None
    row_spec = lambda: pl.BlockSpec((bm, d), lambda i, j: (i, 0))
    gain_spec = pl.BlockSpec((1, d), lambda i, j: (0, 0))
    in_specs = [row_spec(), pl.BlockSpec((d, bn), lambda i, j: (0, j)), row_spec(), gain_spec]
    out_specs = [row_spec()]
    out_shape = [jax.ShapeDtypeStruct((m, d), F32)]
    args = [y, w_o, x2d, gain.reshape(1, d)]
    blocks = bm * d * 2 + d * bn * 2 + 2 * bm * d * 4
    if with_next:
        in_specs.append(gain_spec)
        out_specs.append(row_spec())
        out_shape.append(jax.ShapeDtypeStruct((m, d), BF16))
        args.append(next_gain.reshape(1, d))
        blocks += bm * d * 2
    outs = pl.pallas_call(
        functools.partial(_out_kernel, bn=bn, with_next=with_next),
        grid=(m // bm, d // bn),
        in_specs=in_specs, out_specs=out_specs, out_shape=out_shape,
        compiler_params=pltpu.CompilerParams(
            dimension_semantics=("arbitrary", "arbitrary"),
            vmem_limit_bytes=_vmem_limit(blocks)),
        name="out_proj_norm_residual",
    )(*args)
    return (outs[0], outs[1]) if with_next else (outs[0], None)


def kernel(x, pre_norm_gain, post_norm_gain, w_in, b_merge_gate, conv_w,
           w_branch_a, w_branch_b, w_branch_c, w_out):
    b, s, d_model = x.shape
    depth = w_in.shape[0]
    width = conv_w.shape[-1]
    hw = width // HEAD_DIM
    assert s % MOBA_BLOCK == 0 and d_model % 128 == 0
    tables = _rope_tables(s)
    x2d = x.reshape(b * s, d_model)

    h = _rmsnorm(x2d, pre_norm_gain[0])
    first_rows, w_in_bf16 = _in_proj_first_rows(h, w_in, 0)
    for layer in range(depth):
        last = layer + 1 == depth
        casts = [(w_out, layer)] + ([] if last else [(w_in, layer + 1)])
        proj, cast_outs = _in_proj(h, w_in_bf16, casts, first_rows)
        first_rows = None
        w_o = cast_outs[0]
        if not last:
            w_in_bf16 = cast_outs[1]

        ua = _moba(proj, tables, b, s, width, 0 * hw, 1 * hw, 2 * hw, 3 * hw)
        ub = _conv_branch(proj, conv_w[layer], b, s, width, 4, 5, 6, 7)
        uc = _stick(proj, b, s, width, 8 * hw, 9 * hw, 10 * hw, 11 * hw)
        y = _merge(ua, ub, uc, w_branch_a, w_branch_b, w_branch_c, layer, proj,
                   b_merge_gate[layer].reshape(1, -1), d_model, 12 * width)
        x2d, h = _out_proj(y, w_o, x2d, post_norm_gain[layer],
                           None if last else pre_norm_gain[layer + 1])
    return x2d.reshape(b, s, d_model)
```

```python
import functools
import math

import jax
import jax.numpy as jnp
from jax import lax
from jax.experimental import pallas as pl
from jax.experimental.pallas import tpu as pltpu

HEAD_DIM = 128
MOBA_BLOCK = 256
MOBA_TOPK = 3
ROPE_THETA = 500000.0
ROPE_DIM = HEAD_DIM // 4
CONV_WIDTH = 3
N_BRANCHES = 3
RMS_EPS = 1e-6

VMEM_REQUEST_CAP = 60000 * 1024
SUBLANES = 8

RMSNORM_ROW_TILE = 256
IN_PROJ_TILE = 1024
STICK_TILE = 256
CONV_TILE = 512
MERGE_TILE = 512
OUT_PROJ_TILE = 512
NORM_SLAB_ROWS = 64

F32 = jnp.float32
BF16 = jnp.bfloat16


def _vmem_limit(block_bytes):
    return int(min(VMEM_REQUEST_CAP, max(32 * 1024 * 1024, 2 * block_bytes + 8 * 1024 * 1024)))


def _tile(n, target):
    if n <= target:
        return n
    t = target - target % 128
    while t >= 128:
        if n % t == 0:
            return t
        t -= 128
    return n


def _sigmoid(x):
    return 1.0 / (1.0 + jnp.exp(-x))


def _silu(x):
    return x * _sigmoid(x)


def _rmsnorm_kernel(x_ref, g_ref, o_ref):
    x = x_ref[...]
    inv = lax.rsqrt(jnp.mean(x * x, axis=-1, keepdims=True) + RMS_EPS)
    o_ref[...] = (x * inv * g_ref[...]).astype(o_ref.dtype)


def _rmsnorm(x2d, gain):
    m, d = x2d.shape
    bm = _tile(m, RMSNORM_ROW_TILE)
    return pl.pallas_call(
        _rmsnorm_kernel,
        grid=(m // bm,),
        in_specs=[pl.BlockSpec((bm, d), lambda i: (i, 0)),
                  pl.BlockSpec((1, d), lambda i: (0, 0))],
        out_specs=pl.BlockSpec((bm, d), lambda i: (i, 0)),
        out_shape=jax.ShapeDtypeStruct((m, d), BF16),
        compiler_params=pltpu.CompilerParams(dimension_semantics=("arbitrary",)),
        name="rmsnorm_pre",
    )(x2d, gain.reshape(1, d))


def _rope_tables(s):
    half = ROPE_DIM // 2
    inv_freq = ROPE_THETA ** (-jnp.arange(half, dtype=F32) / half)
    ang = jnp.arange(s).astype(F32)[:, None] * inv_freq[None, :]
    cos = jnp.cos(ang)
    sin = jnp.sin(ang)
    rest = HEAD_DIM - ROPE_DIM
    zeros_h = jnp.zeros((s, half), F32)
    cos_t = jnp.concatenate([cos, cos, jnp.ones((s, rest), F32)], axis=-1)
    sin_up = jnp.concatenate([zeros_h, sin, jnp.zeros((s, rest), F32)], axis=-1)
    sin_dn = jnp.concatenate([-sin, zeros_h, jnp.zeros((s, rest), F32)], axis=-1)
    return cos_t, sin_up, sin_dn


def _rope(x, cos_t, sin_up, sin_dn):
    half = ROPE_DIM // 2
    return (x * cos_t + pltpu.roll(x, half, 1) * sin_up
            + pltpu.roll(x, HEAD_DIM - half, 1) * sin_dn)


def _in_proj_kernel(x_ref, w_ref, *refs, n_casts, has_first, n_col_tiles):
    cast_srcs = refs[:n_casts]
    o_ref = refs[n_casts + has_first]
    cast_dsts = refs[n_casts + has_first + 1:n_casts + has_first + 1 + n_casts]
    for src, dst in zip(cast_srcs, cast_dsts):
        dst[...] = src[...].astype(BF16)

    i, j = pl.program_id(0), pl.program_id(1)
    in_range = j < n_col_tiles
    if has_first:
        first_hbm, sem = refs[n_casts], refs[-1]
        bn = o_ref.shape[1]

        @pl.when(jnp.logical_and(in_range, i == 0))
        def _copy_first_rows():
            cols = pl.ds(pl.multiple_of(j * bn, bn), bn)
            copy = pltpu.make_async_copy(first_hbm.at[:, cols], o_ref, sem)
            copy.start()
            copy.wait()
        in_range = jnp.logical_and(in_range, i > 0)

    @pl.when(in_range)
    def _project():
        o_ref[...] = jnp.dot(x_ref[...], w_ref[...], preferred_element_type=F32)


BF16_ROW_TILE = 16
LANES = 128


def _cast_blocks(r, wdt, max_blocks):
    options = [(r, c) for c in range(LANES, wdt + 1, LANES) if wdt % c == 0 and wdt // c <= max_blocks]
    options += [(q, wdt) for q in range(BF16_ROW_TILE, r + 1, BF16_ROW_TILE)
                if r % q == 0 and r // q <= max_blocks]
    return min(options, key=lambda rc: rc[0] * rc[1])


def _in_proj(h, w_bf16, casts, first_rows=None):
    m, k = h.shape
    _, n = w_bf16.shape
    bm = _tile(m, IN_PROJ_TILE)
    bn = _tile(n, IN_PROJ_TILE)
    has_first = first_rows is not None
    n_rows = m // bm
    nj = n // bn
    max_cols = nj + nj // 4
    plans = [_cast_blocks(st.shape[1], st.shape[2], n_rows * max_cols) for st, _ in casts]
    n_blocks = [(st.shape[1] // rb) * (st.shape[2] // cb) for (st, _), (rb, cb) in zip(casts, plans)]
    grid_cols = max([nj] + [-(-nb // n_rows) for nb in n_blocks])

    col = lambda j: jnp.minimum(j, nj - 1)
    w_col = (lambda i, j: jnp.where(i == 0, 0, col(j))) if has_first else (lambda i, j: col(j))
    in_specs = [pl.BlockSpec((bm, k), lambda i, j: (i, 0)),
                pl.BlockSpec((k, bn), lambda i, j: (0, w_col(i, j)))]
    out_specs = [pl.BlockSpec((bm, bn), lambda i, j: (i, col(j)))]
    out_shape = [jax.ShapeDtypeStruct((m, n), F32)]
    args = [h, w_bf16]
    for (stack, layer), (rb, cb), nb in zip(casts, plans, n_blocks):
        _, r, wdt = stack.shape
        by_cols = rb == r

        def block_index(i, j, nb=nb, by_cols=by_cols):
            t = jnp.minimum(i * grid_cols + j, nb - 1)
            return (0, t) if by_cols else (t, 0)

        in_specs.append(pl.BlockSpec(
            (None, rb, cb), lambda i, j, layer=layer, f=block_index: (layer, *f(i, j))))
        out_specs.append(pl.BlockSpec((rb, cb), block_index))
        out_shape.append(jax.ShapeDtypeStruct((r, wdt), BF16))
        args.append(stack)
    if has_first:
        in_specs.append(pl.BlockSpec(memory_space=pl.ANY))
        args.append(first_rows)
    outs = pl.pallas_call(
        functools.partial(_in_proj_kernel, n_casts=len(casts), has_first=int(has_first),
                          n_col_tiles=nj),
        grid=(n_rows, grid_cols),
        in_specs=in_specs, out_specs=out_specs, out_shape=out_shape,
        scratch_shapes=[pltpu.SemaphoreType.DMA(())] if has_first else [],
        compiler_params=pltpu.CompilerParams(
            dimension_semantics=("arbitrary", "arbitrary"),
            vmem_limit_bytes=VMEM_REQUEST_CAP),
        name="in_proj",
    )(*args)
    return outs[0], outs[1:]


def _in_proj_first_rows_kernel(x_ref, w_ref, o_ref, wb_ref):
    wb = w_ref[...].astype(BF16)
    wb_ref[...] = wb
    o_ref[...] = jnp.dot(x_ref[...], wb, preferred_element_type=F32)


def _in_proj_first_rows(h, w_stack, layer):
    m, k = h.shape
    _, _, n = w_stack.shape
    bm = _tile(m, IN_PROJ_TILE)
    bn = _tile(n, IN_PROJ_TILE // 2)
    blocks = bm * k * 2 + k * bn * 4 + k * bn * 2 + bm * bn * 4
    return pl.pallas_call(
        _in_proj_first_rows_kernel,
        grid=(n // bn,),
        in_specs=[pl.BlockSpec((bm, k), lambda j: (0, 0)),
                  pl.BlockSpec((None, k, bn), lambda j: (layer, 0, j))],
        out_specs=[pl.BlockSpec((bm, bn), lambda j: (0, j)),
                   pl.BlockSpec((k, bn), lambda j: (0, j))],
        out_shape=[jax.ShapeDtypeStruct((bm, n), F32), jax.ShapeDtypeStruct((k, n), BF16)],
        compiler_params=pltpu.CompilerParams(
            dimension_semantics=("arbitrary",),
            vmem_limit_bytes=_vmem_limit(blocks)),
        name="in_proj_first_rows",
    )(h, w_stack)


_TRANS_B = (((1,), (1,)), ((), ()))


MOBA_HEADS_PER_STEP = 2
STICK_HEADS_PER_STEP = 4
MOBA_CHUNK_BLOCKS = 4
LOG2_E = 1.4426950408889634


def _head_cols(g):
    return slice(g * HEAD_DIM, (g + 1) * HEAD_DIM)


def _moba_kernel(q_ref, k_ref, v_ref, z_ref, cq_ref, uq_ref, dq_ref, ck_ref, uk_ref, dk_ref,
                 o_ref, kr_scr, vt_blk_scr, vt_chunk_scr, kmean_scr, bias_scr, s0_scr, s1_scr,
                 *, n_blk, tq, n_grp, n_cb):
    i = pl.program_id(2)
    blk = MOBA_BLOCK
    chunk = n_cb * blk
    scale2 = HEAD_DIM ** -0.5 * LOG2_E
    neg_inf = -jnp.inf
    heads = range(n_grp)

    @pl.when(i == 0)
    def _prepare_keys():
        def one_chunk(c, carry):
            for u in range(n_cb):
                j = c * n_cb + u
                rows = pl.ds(pl.multiple_of(j * blk, blk), blk)
                ck, uk, dk = ck_ref[rows, :], uk_ref[rows, :], dk_ref[rows, :]
                for g in heads:
                    kr = _rope(k_ref[rows, _head_cols(g)], ck, uk, dk)
                    kr_scr[g, rows, :] = kr.astype(BF16)
                    vt = v_ref[rows, _head_cols(g)].T.astype(BF16)
                    vt_blk_scr[g, j] = vt
                    vt_chunk_scr[g, c, :, u * blk:(u + 1) * blk] = vt
                    kmean_scr[g, pl.ds(j, 1), :] = jnp.mean(kr, axis=0, keepdims=True)
            return carry
        lax.fori_loop(0, n_blk // n_cb, one_chunk, 0)

    own = i
    own_rows = pl.ds(pl.multiple_of(own * blk, blk), blk)
    blk_id = lax.broadcasted_iota(jnp.int32, (n_blk, tq), 0).astype(F32)
    past = blk_id < own.astype(F32)
    causal = (lax.broadcasted_iota(jnp.int32, (blk, tq), 0)
              <= lax.broadcasted_iota(jnp.int32, (blk, tq), 1))

    cq, uq, dq = cq_ref[...], uq_ref[...], dq_ref[...]
    qs = [_rope(q_ref[:, _head_cols(g)], cq, uq, dq) for g in heads]
    qbs = [(q * scale2).astype(BF16) for q in qs]
    ss = [lax.dot_general(kr_scr[g, own_rows, :], qbs[g], _TRANS_B, preferred_element_type=F32)
          for g in heads]
    gates = [lax.dot_general(kmean_scr[g], qs[g], _TRANS_B, precision=lax.Precision.HIGHEST,
                             preferred_element_type=F32) for g in heads]
    gscs = [jnp.where(past, gate, neg_inf) for gate in gates]
    sels = [jnp.zeros((n_blk, tq), jnp.bool_) for _ in heads]
    for _ in range(min(MOBA_TOPK, n_blk)):
        gmaxs = [jnp.max(gsc, axis=0, keepdims=True) for gsc in gscs]
        firsts = [jnp.min(jnp.where(gscs[g] == gmaxs[g], blk_id, float(n_blk)), axis=0,
                          keepdims=True) for g in heads]
        picks = [blk_id == first for first in firsts]
        sels = [jnp.logical_or(sels[g], picks[g]) for g in heads]
        gscs = [jnp.where(picks[g], neg_inf, gscs[g]) for g in heads]
    for g in heads:
        bias_scr[g] = jnp.where(jnp.logical_and(sels[g], past), 0.0, neg_inf)

    ss = [jnp.where(causal, s, neg_inf) for s in ss]
    ms = [jnp.max(s, axis=0, keepdims=True) for s in ss]
    ps = [jnp.exp2(ss[g] - ms[g]) for g in heads]
    ls = [jnp.sum(p, axis=0, keepdims=True) for p in ps]
    accs = [jnp.dot(vt_blk_scr[g, own], ps[g].astype(BF16), preferred_element_type=F32)
            for g in heads]

    n_chunks = (own + n_cb - 1) // n_cb
    last_chunk = n_blk // n_cb - 1

    def score_chunk(c, dst_scr):
        c = jnp.minimum(c, last_chunk)
        rows = pl.ds(pl.multiple_of(c * chunk, chunk), chunk)
        for g in heads:
            sc = lax.dot_general(kr_scr[g, rows, :], qbs[g], _TRANS_B,
                                 preferred_element_type=F32)
            for u in range(n_cb):
                dst_scr[g, u * blk:(u + 1) * blk, :] = (
                    sc[u * blk:(u + 1) * blk] + bias_scr[g, pl.ds(c * n_cb + u, 1), :])

    def absorb_chunk(c, src_scr, state):
        ms, ls, accs = state
        m_news, new_ls, new_accs = [], [], []
        for g in heads:
            s = src_scr[g]
            m_new = jnp.maximum(ms[g], jnp.max(s, axis=0, keepdims=True))
            alpha = jnp.exp2(ms[g] - m_new)
            p = jnp.exp2(s - m_new)
            new_ls.append(alpha * ls[g] + jnp.sum(p, axis=0, keepdims=True))
            new_accs.append(alpha * accs[g] + jnp.dot(vt_chunk_scr[g, c], p.astype(BF16),
                                                      preferred_element_type=F32))
            m_news.append(m_new)
        return tuple(m_news), tuple(new_ls), tuple(new_accs)

    score_chunk(0, s0_scr)

    def chunk_pair(p, state):
        c = 2 * p
        score_chunk(c + 1, s1_scr)
        state = absorb_chunk(c, s0_scr, state)

        def second(state):
            score_chunk(c + 2, s0_scr)
            return absorb_chunk(c + 1, s1_scr, state)

        return lax.cond(c + 1 < n_chunks, second, lambda st: st, state)

    _, ls, accs = lax.fori_loop(0, (n_chunks + 1) // 2, chunk_pair,
                                (tuple(ms), tuple(ls), tuple(accs)))
    for g in heads:
        out = (accs[g] * (1.0 / ls[g])).T
        o_ref[:, _head_cols(g)] = (out * _silu(z_ref[:, _head_cols(g)])).astype(o_ref.dtype)


def _moba(proj, tables, b, s, width, col_q, col_k, col_v, col_z):
    n_grp = MOBA_HEADS_PER_STEP
    n_steps = width // (HEAD_DIM * n_grp)
    n_blk = s // MOBA_BLOCK
    n_cb = MOBA_CHUNK_BLOCKS
    assert n_blk % n_cb == 0
    tq = MOBA_BLOCK
    nq = s // tq
    d = HEAD_DIM
    dg = d * n_grp
    assert col_q % n_grp == 0 and col_k % n_grp == 0 and col_v % n_grp == 0 and col_z % n_grp == 0

    q_spec = lambda off: pl.BlockSpec((tq, dg), lambda bi, h, i: (bi * nq + i, off // n_grp + h))
    kv_spec = lambda off: pl.BlockSpec((s, dg), lambda bi, h, i: (bi, off // n_grp + h))
    tq_spec = pl.BlockSpec((tq, d), lambda bi, h, i: (i, 0))
    tk_spec = pl.BlockSpec((s, d), lambda bi, h, i: (0, 0))
    blocks = (2 * s * dg * 4 + 3 * s * d * 4 + 2 * tq * dg * 4 + 3 * tq * d * 4 + tq * dg * 2
              + 3 * s * dg * 2)
    return pl.pallas_call(
        functools.partial(_moba_kernel, n_blk=n_blk, tq=tq, n_grp=n_grp, n_cb=n_cb),
        grid=(b, n_steps, nq),
        in_specs=[q_spec(col_q), kv_spec(col_k), kv_spec(col_v), q_spec(col_z),
                  tq_spec, tq_spec, tq_spec, tk_spec, tk_spec, tk_spec],
        out_specs=pl.BlockSpec((tq, dg), lambda bi, h, i: (bi * nq + i, h)),
        out_shape=jax.ShapeDtypeStruct((b * s, width), BF16),
        scratch_shapes=[pltpu.VMEM((n_grp, s, d), BF16),
                        pltpu.VMEM((n_grp, n_blk, d, MOBA_BLOCK), BF16),
                        pltpu.VMEM((n_grp, n_blk // n_cb, d, n_cb * MOBA_BLOCK), BF16),
                        pltpu.VMEM((n_grp, n_blk, d), F32),
                        pltpu.VMEM((n_grp, n_blk, tq), F32),
                        pltpu.VMEM((n_grp, n_cb * MOBA_BLOCK, tq), F32),
                        pltpu.VMEM((n_grp, n_cb * MOBA_BLOCK, tq), F32)],
        compiler_params=pltpu.CompilerParams(
            dimension_semantics=("arbitrary", "arbitrary", "arbitrary"),
            vmem_limit_bytes=_vmem_limit(blocks + n_grp * n_cb * MOBA_BLOCK * tq * 4)),
        name="moba_attention",
    )(proj, proj, proj, proj, *tables, *tables)


F32_EXP2_ZERO_BELOW = -152.0


def _softplus2(z2):
    return jnp.maximum(z2, 0.0) + jnp.log2(1.0 + jnp.exp2(-jnp.abs(z2)))


def _suffix_sums(tri, x):
    return jnp.dot(tri, x.astype(BF16), preferred_element_type=F32)


def _stick_kernel(q_ref, k_ref, v_ref, z_ref, tri_ref, o_ref, kb_scr, vt_scr, *, tq, n_grp):
    i = pl.program_id(2)
    scale2 = HEAD_DIM ** -0.5 * LOG2_E
    n_tiles = k_ref.shape[0] // tq
    heads = range(n_grp)

    @pl.when(i == 0)
    def _prepare_keys():
        def one_tile(j, carry):
            rows = pl.ds(pl.multiple_of(j * tq, tq), tq)
            for g in heads:
                kb_scr[g, rows, :] = k_ref[rows, _head_cols(g)].astype(BF16)
                vt_scr[g, j] = v_ref[rows, _head_cols(g)].T.astype(BF16)
            return carry
        lax.fori_loop(0, n_tiles, one_tile, 0)

    tri = tri_ref[...]
    qbs = [(q_ref[:, _head_cols(g)] * scale2).astype(BF16) for g in heads]
    causal = (lax.broadcasted_iota(jnp.int32, (tq, tq), 0)
              < lax.broadcasted_iota(jnp.int32, (tq, tq), 1))

    rows = pl.ds(pl.multiple_of(i * tq, tq), tq)
    zs = [lax.dot_general(kb_scr[g, rows, :], qbs[g], _TRANS_B, preferred_element_type=F32)
          for g in heads]
    sps = [_softplus2(z) for z in zs]
    drops = [jnp.where(causal, sp, 0.0) for sp in sps]
    laters = [_suffix_sums(tri, dr) for dr in drops]
    removed = [laters[g][0:1, :] + drops[g][0:1, :] for g in heads]
    a = [jnp.where(causal, jnp.exp2(zs[g] - sps[g] - laters[g]), 0.0).astype(BF16) for g in heads]
    accs = [jnp.dot(vt_scr[g, i], a[g], preferred_element_type=F32) for g in heads]

    def still_needed(removed):
        least = removed[0]
        for r in removed[1:]:
            least = jnp.minimum(least, r)
        return (-jnp.min(least) > F32_EXP2_ZERO_BELOW).astype(jnp.int32)

    def more_tiles(state):
        t, go = state[0], state[1]
        return jnp.logical_and(t < i, go > 0)

    def earlier_tile(state):
        t, _, removed, accs = state
        j = i - 1 - t
        rows = pl.ds(pl.multiple_of(j * tq, tq), tq)
        zs = [lax.dot_general(kb_scr[g, rows, :], qbs[g], _TRANS_B, preferred_element_type=F32)
              for g in heads]
        sps = [_softplus2(z) for z in zs]
        within = [_suffix_sums(tri, sp) for sp in sps]
        new_removed = [removed[g] + within[g][0:1, :] + sps[g][0:1, :] for g in heads]
        a = [jnp.exp2(zs[g] - sps[g] - within[g] - removed[g]).astype(BF16) for g in heads]
        new_accs = [accs[g] + jnp.dot(vt_scr[g, j], a[g], preferred_element_type=F32)
                    for g in heads]
        return (t + 1, still_needed(new_removed), tuple(new_removed), tuple(new_accs))

    state = lax.while_loop(more_tiles, earlier_tile,
                           (jnp.int32(0), jnp.int32(1), tuple(removed), tuple(accs)))
    for g in heads:
        o_ref[:, _head_cols(g)] = (state[3][g].T * _silu(z_ref[:, _head_cols(g)])).astype(o_ref.dtype)


def _stick(proj, b, s, width, col_q, col_k, col_v, col_z):
    n_grp = math.gcd(STICK_HEADS_PER_STEP, width // HEAD_DIM)
    n_steps = width // (HEAD_DIM * n_grp)
    tq = STICK_TILE
    assert s % tq == 0
    nq = s // tq
    d = HEAD_DIM
    dg = d * n_grp
    assert col_q % n_grp == 0 and col_k % n_grp == 0 and col_v % n_grp == 0 and col_z % n_grp == 0
    tri = (jnp.arange(tq)[None, :] > jnp.arange(tq)[:, None]).astype(BF16)

    q_spec = lambda off: pl.BlockSpec((tq, dg), lambda bi, h, i: (bi * nq + i, off // n_grp + h))
    kv_spec = lambda off: pl.BlockSpec((s, dg), lambda bi, h, i: (bi, off // n_grp + h))
    blocks = 2 * s * dg * 4 + 3 * tq * dg * 4 + tq * tq * 2 + 2 * s * dg * 2
    return pl.pallas_call(
        functools.partial(_stick_kernel, tq=tq, n_grp=n_grp),
        grid=(b, n_steps, nq),
        in_specs=[q_spec(col_q), kv_spec(col_k), kv_spec(col_v), q_spec(col_z),
                  pl.BlockSpec((tq, tq), lambda bi, h, i: (0, 0))],
        out_specs=pl.BlockSpec((tq, dg), lambda bi, h, i: (bi * nq + i, h)),
        out_shape=jax.ShapeDtypeStruct((b * s, width), BF16),
        scratch_shapes=[pltpu.VMEM((n_grp, s, d), BF16),
                        pltpu.VMEM((n_grp, s // tq, d, tq), BF16)],
        compiler_params=pltpu.CompilerParams(
            dimension_semantics=("arbitrary", "arbitrary", "arbitrary"),
            vmem_limit_bytes=_vmem_limit(blocks)),
        name="stick_breaking_attention",
    )(proj, proj, proj, proj, tri)


def _conv_kernel(bb_ref, bc_ref, bx_ref, bz_ref, hc_ref, hx_ref, w_ref, o_ref):
    i = pl.program_id(1)
    xc = bc_ref[...] * bx_ref[...]
    halo = jnp.where(i == 0, 0.0, hc_ref[...] * hx_ref[...])
    full = jnp.concatenate([halo, xc], axis=0)
    prev1 = pltpu.roll(full, 1, 0)[SUBLANES:]
    prev2 = pltpu.roll(full, 2, 0)[SUBLANES:]
    w = w_ref[...]
    y = w[0:1, :] * prev2 + w[1:2, :] * prev1 + w[2:3, :] * xc
    o_ref[...] = (bb_ref[...] * y * _silu(bz_ref[...])).astype(o_ref.dtype)


def _conv_branch(proj, conv_w, b, s, width, col_bb, col_bc, col_bx, col_bz):
    ts = _tile(s, CONV_TILE)
    tw = _tile(width, CONV_TILE)
    ns = s // ts
    nw = width // tw
    halo_blocks = ts // SUBLANES

    main = lambda off: pl.BlockSpec((ts, tw), lambda bi, i, j: (bi * ns + i, off * nw + j))
    halo = lambda off: pl.BlockSpec(
        (SUBLANES, tw),
        lambda bi, i, j: (jnp.maximum((bi * ns + i) * halo_blocks - 1, 0), off * nw + j))
    return pl.pallas_call(
        _conv_kernel,
        grid=(b, ns, nw),
        in_specs=[main(col_bb), main(col_bc), main(col_bx), main(col_bz),
                  halo(col_bc), halo(col_bx),
                  pl.BlockSpec((CONV_WIDTH, tw), lambda bi, i, j: (0, j))],
        out_specs=pl.BlockSpec((ts, tw), lambda bi, i, j: (bi * ns + i, j)),
        out_shape=jax.ShapeDtypeStruct((b * s, width), BF16),
        compiler_params=pltpu.CompilerParams(
            dimension_semantics=("arbitrary", "arbitrary", "arbitrary")),
        name="short_conv_branch",
    )(proj, proj, proj, proj, proj, proj, conv_w)


def _merge_kernel(ua_ref, ub_ref, uc_ref, pa_ref, pb_ref, pc_ref,
                  ga_ref, gb_ref, gc_ref, ba_ref, bb_ref, bc_ref, o_ref,
                  pa_scr, pb_scr, pc_scr):
    @pl.when(pl.program_id(1) == 0)
    def _cast_weights():
        pa_scr[...] = pa_ref[...].astype(BF16)
        pb_scr[...] = pb_ref[...].astype(BF16)
        pc_scr[...] = pc_ref[...].astype(BF16)

    def branch(u_ref, p_scr, g_ref, b_ref):
        gate = _sigmoid(g_ref[...] + b_ref[...])
        return gate * jnp.dot(u_ref[...], p_scr[...], preferred_element_type=F32)
    y = (branch(ua_ref, pa_scr, ga_ref, ba_ref) + branch(ub_ref, pb_scr, gb_ref, bb_ref)
         + branch(uc_ref, pc_scr, gc_ref, bc_ref))
    o_ref[...] = y.astype(o_ref.dtype)


def _merge(ua, ub, uc, pa, pb, pc, layer, proj, b_merge, d_model, gate_col):
    m, width = ua.shape
    bm = _tile(m, MERGE_TILE)
    bn = _tile(d_model, MERGE_TILE)
    nn = d_model // bn
    assert gate_col % bn == 0
    col_g = gate_col // bn

    u_spec = pl.BlockSpec((bm, width), lambda j, i: (i, 0))
    p_spec = pl.BlockSpec((None, width, bn), lambda j, i: (layer, 0, j))
    g_spec = lambda br: pl.BlockSpec((bm, bn), lambda j, i: (i, col_g + br * nn + j))
    b_spec = lambda br: pl.BlockSpec((1, bn), lambda j, i: (0, br * nn + j))
    blocks = 3 * bm * width * 2 + 3 * width * bn * 4 + 3 * bm * bn * 4 + bm * bn * 2
    return pl.pallas_call(
        _merge_kernel,
        grid=(nn, m // bm),
        in_specs=[u_spec, u_spec, u_spec, p_spec, p_spec, p_spec,
                  g_spec(0), g_spec(1), g_spec(2), b_spec(0), b_spec(1), b_spec(2)],
        out_specs=pl.BlockSpec((bm, bn), lambda j, i: (i, j)),
        out_shape=jax.ShapeDtypeStruct((m, d_model), BF16),
        scratch_shapes=[pltpu.VMEM((width, bn), BF16)] * N_BRANCHES,
        compiler_params=pltpu.CompilerParams(
            dimension_semantics=("arbitrary", "arbitrary"),
            vmem_limit_bytes=_vmem_limit(blocks + 3 * width * bn)),
        name="gated_merge",
    )(ua, ub, uc, pa, pb, pc, proj, proj, proj, b_merge, b_merge, b_merge)


def _out_kernel(y_ref, w_hbm, x_ref, g_ref, *refs, bn, with_next):
    w_ring, sem = refs[-2], refs[-1]
    outs = refs[:-2]
    o_ref = outs[-2] if with_next else outs[-1]
    i, j = pl.program_id(0), pl.program_id(1)
    n_rows, n_cols = pl.num_programs(0), pl.num_programs(1)
    t = i * n_cols + j
    total = n_rows * n_cols

    def tile_copy(step):
        col = lax.rem(step, n_cols)
        slot = lax.rem(step, OUT_PROJ_RING)
        return pltpu.make_async_copy(
            w_hbm.at[:, pl.ds(pl.multiple_of(col * bn, bn), bn)], w_ring.at[slot], sem.at[slot])

    @pl.when(t == 0)
    def _prime():
        for step in range(OUT_PROJ_RING - 1):
            tile_copy(step).start()

    @pl.when(t + OUT_PROJ_RING - 1 < total)
    def _fetch_ahead():
        tile_copy(t + OUT_PROJ_RING - 1).start()

    tile_copy(t).wait()
    cols = pl.ds(pl.multiple_of(j * bn, bn), bn)
    o_ref[:, cols] = jnp.dot(y_ref[...], w_ring[lax.rem(t, OUT_PROJ_RING)],
                             preferred_element_type=F32)

    @pl.when(j == n_cols - 1)
    def _normalize():
        def one_slab(r, carry):
            rows = pl.ds(pl.multiple_of(r * NORM_SLAB_ROWS, NORM_SLAB_ROWS), NORM_SLAB_ROWS)
            out = o_ref[rows, :]
            inv = lax.rsqrt(jnp.mean(out * out, axis=-1, keepdims=True) + RMS_EPS)
            x_new = x_ref[rows, :] + out * inv * g_ref[...]
            o_ref[rows, :] = x_new
            if with_next:
                gn_ref, h_ref = outs[0], outs[-1]
                inv_n = lax.rsqrt(jnp.mean(x_new * x_new, axis=-1, keepdims=True) + RMS_EPS)
                h_ref[rows, :] = (x_new * inv_n * gn_ref[...]).astype(h_ref.dtype)
            return carry
        lax.fori_loop(0, o_ref.shape[0] // NORM_SLAB_ROWS, one_slab, 0)


OUT_PROJ_RING = 3


def _out_proj(y, w_o, x2d, gain, next_gain=None):
    m, d = y.shape
    bm = _tile(m, OUT_PROJ_TILE)
    bn = _tile(d, OUT_PROJ_TILE)
    with_next = next_gain is not None
    assert (m // bm) * (d // bn) >= OUT_PROJ_RING
    row_spec = lambda: pl.BlockSpec((bm, d), lambda i, j: (i, 0))
    gain_spec = pl.BlockSpec((1, d), lambda i, j: (0, 0))
    y_spec = (pl.BlockSpec((bm, d), lambda i, j: (i, 0), pipeline_mode=pl.Buffered(1))
              if with_next else row_spec())
    in_specs = [y_spec, pl.BlockSpec(memory_space=pl.ANY), row_spec(), gain_spec]
    out_specs = [row_spec()]
    out_shape = [jax.ShapeDtypeStruct((m, d), F32)]
    args = [y, w_o, x2d, gain.reshape(1, d)]
    if with_next:
        in_specs.append(gain_spec)
        out_specs.append(row_spec())
        out_shape.append(jax.ShapeDtypeStruct((m, d), BF16))
        args.append(next_gain.reshape(1, d))
    outs = pl.pallas_call(
        functools.partial(_out_kernel, bn=bn, with_next=with_next),
        grid=(m // bm, d // bn),
        in_specs=in_specs, out_specs=out_specs, out_shape=out_shape,
        scratch_shapes=[pltpu.VMEM((OUT_PROJ_RING, d, bn), BF16),
                        pltpu.SemaphoreType.DMA((OUT_PROJ_RING,))],
        compiler_params=pltpu.CompilerParams(
            dimension_semantics=("arbitrary", "arbitrary"),
            vmem_limit_bytes=VMEM_REQUEST_CAP),
        name="out_proj_norm_residual",
    )(*args)
    return (outs[0], outs[1]) if with_next else (outs[0], None)


def kernel(x, pre_norm_gain, post_norm_gain, w_in, b_merge_gate, conv_w,
           w_branch_a, w_branch_b, w_branch_c, w_out):
    b, s, d_model = x.shape
    depth = w_in.shape[0]
    width = conv_w.shape[-1]
    hw = width // HEAD_DIM
    assert s % MOBA_BLOCK == 0 and d_model % 128 == 0
    tables = _rope_tables(s)
    x2d = x.reshape(b * s, d_model)

    h = _rmsnorm(x2d, pre_norm_gain[0])
    first_rows, w_in_bf16 = _in_proj_first_rows(h, w_in, 0)
    for layer in range(depth):
        last = layer + 1 == depth
        casts = [(w_out, layer)] + ([] if last else [(w_in, layer + 1)])
        proj, cast_outs = _in_proj(h, w_in_bf16, casts, first_rows)
        first_rows = None
        w_o = cast_outs[0]
        if not last:
            w_in_bf16 = cast_outs[1]

        ua = _moba(proj, tables, b, s, width, 0 * hw, 1 * hw, 2 * hw, 3 * hw)
        ub = _conv_branch(proj, conv_w[layer], b, s, width, 4, 5, 6, 7)
        uc = _stick(proj, b, s, width, 8 * hw, 9 * hw, 10 * hw, 11 * hw)
        y = _merge(ua, ub, uc, w_branch_a, w_branch_b, w_branch_c, layer, proj,
                   b_merge_gate[layer].reshape(1, -1), d_model, 12 * width)
        x2d, h = _out_proj(y, w_o, x2d, post_norm_gain[layer],
                           None if last else pre_norm_gain[layer + 1])
    return x2d.reshape(b, s, d_model)
```
